```python
import math
import jax
import jax.numpy as jnp
from jax import lax
import numpy as np

D_MODEL = 1024
BATCH = 16
SEQ = 2048
DEPTH = 2
DEC_BATCH = 128
DEC_SEQ = 8
PAST_LEN = 8192
PAGE_SIZE = 128

ROPE_THETA = 10000.0
EPS = 1e-6
Q_BLOCK = 128
D_FF = 4 * D_MODEL
N_BRANCH = 4
BRANCH_WIDTH = 256

NSA_HEADS = 4
NSA_DK = 64
CMP_BLOCK = 32
CMP_STRIDE = 16
CMP_HIDDEN = 64
SLC_BLOCK = 64
SLC_TOPN = 16
WINDOW = 512
MLA_HEADS = 4
MLA_KV_RANK = 256
MLA_NOPE = 64
MLA_ROPE = 32
MLA_V = 64
FOX_HEADS = 4
FOX_KV_HEADS = 2
FOX_DK = 64
DSA_HEADS = 4
DSA_DK = 64
IDX_HEADS = 4
IDX_DK = 64
DSA_TOPK = 256

IN_SPLITS = (
    NSA_HEADS * NSA_DK, NSA_DK, NSA_DK, NSA_DK, NSA_DK, NSA_DK, NSA_DK, 3 * NSA_HEADS,
    MLA_HEADS * MLA_NOPE, MLA_HEADS * MLA_ROPE, MLA_KV_RANK, MLA_ROPE,
    FOX_HEADS * FOX_DK, FOX_KV_HEADS * FOX_DK, FOX_KV_HEADS * FOX_DK, FOX_HEADS,
    DSA_HEADS * DSA_DK, DSA_DK, DSA_DK, IDX_HEADS * IDX_DK, IDX_DK, IDX_HEADS,
    N_BRANCH * D_MODEL,
)

kernel_name = 'hybrid_nsa_mla_fox_dsa_decode_step'


def rms_norm(x, g):
    xf = x.astype(jnp.float32)
    y = xf * lax.rsqrt(jnp.mean(xf * xf, axis=-1, keepdims=True) + EPS)
    return (y * g).astype(x.dtype)


def rope(x, pos):
    half = x.shape[-1] // 2
    inv = ROPE_THETA ** (-jnp.arange(half, dtype=jnp.float32) / half)
    ang = pos.astype(jnp.float32)[:, None] * inv[None, :]
    if x.ndim == 4:
        ang = ang[:, None, :]
    cos, sin = jnp.cos(ang), jnp.sin(ang)
    xf = x.astype(jnp.float32)
    x1, x2 = xf[..., :half], xf[..., half:]
    return jnp.concatenate([x1 * cos - x2 * sin, x2 * cos + x1 * sin], axis=-1).astype(x.dtype)


def masked_softmax(s, mask):
    s = jnp.where(mask, s, -jnp.inf)
    m = jnp.max(s, axis=-1, keepdims=True)
    e = jnp.exp(s - jnp.where(jnp.isfinite(m), m, 0.0))
    return e / jnp.maximum(jnp.sum(e, axis=-1, keepdims=True), 1e-30)


def sweep_query_blocks(fn, *xs):
    T = xs[0].shape[1]
    qb = math.gcd(T, Q_BLOCK)
    nb = T // qb

    def split(a):
        return jnp.swapaxes(a.reshape(a.shape[0], nb, qb, *a.shape[2:]), 0, 1)

    out = lax.map(lambda args: fn(*args), tuple(split(a) for a in xs))
    return jnp.swapaxes(out, 0, 1).reshape(out.shape[1], T, *out.shape[3:])


def gather_pages(cache, layer, page_table):
    rows = cache[layer, page_table]
    return rows.reshape(rows.shape[0], -1, rows.shape[-1])


def nsa_compress(x, pe, w1, w2):
    B, L, dk = x.shape
    r = CMP_BLOCK // CMP_STRIDE
    n_ch = -(-L // CMP_STRIDE)
    x = jnp.pad(x, ((0, 0), (0, n_ch * CMP_STRIDE - L), (0, 0)))
    ch = x.reshape(B, n_ch, CMP_STRIDE * dk)
    w1c = w1.reshape(r, CMP_STRIDE * dk, w1.shape[-1])
    n_c = n_ch - r + 1
    h = pe.reshape(-1) @ w1
    for j in range(r):
        h = h + ch[:, j:j + n_c] @ w1c[j]
    return jax.nn.silu(h) @ w2


def nsa_attention(q, gates, pos, k_cmp, v_cmp, k_slc, v_slc, win_ext, pos0, lw):
    B, L, dk = k_cmp.shape
    H = q.shape[2]
    scale = dk ** -0.5
    ck = nsa_compress(k_cmp, lw['nsa_pe_k'], lw['nsa_wk1'], lw['nsa_wk2'])
    cv = nsa_compress(v_cmp, lw['nsa_pe_v'], lw['nsa_wv1'], lw['nsa_wv2'])
    n_c = ck.shape[1]
    cmp_end = jnp.arange(n_c) * CMP_STRIDE + (CMP_BLOCK - 1)
    n_s = -(-L // SLC_BLOCK)
    tail = ((0, 0), (0, n_s * SLC_BLOCK - L), (0, 0))
    ks_b = jnp.pad(k_slc, tail).reshape(B, n_s, SLC_BLOCK, dk)
    vs_b = jnp.pad(v_slc, tail).reshape(B, n_s, SLC_BLOCK, dk)
    n_sel = min(SLC_TOPN, n_s)
    r_s = SLC_BLOCK // CMP_STRIDE
    r_c = CMP_BLOCK // CMP_STRIDE
    wk_ext, wv_ext = win_ext[..., :dk], win_ext[..., dk:]
    b_idx = jnp.arange(B)[:, None, None]
    blk_ids = jnp.arange(n_s)

    def block(qb, gb, pb):
        qp = pb[0]
        nq = qp.shape[0]
        s_c = jnp.einsum('bqhd,bnd->bhqn', qb, ck).astype(jnp.float32) * scale
        p_c = masked_softmax(s_c, cmp_end[None, :] <= qp[:, None])
        o_c = jnp.einsum('bhqn,bnd->bqhd', p_c.astype(cv.dtype), cv)
        imp = jnp.pad(p_c.sum(axis=1), ((0, 0), (0, 0), (r_c - 1, r_s * n_s - n_c)))
        imp_s = imp[..., 0:r_s * n_s:r_s]
        for m in range(1, r_s + r_c - 1):
            imp_s = imp_s + imp[..., m:m + r_s * n_s:r_s]
        cur = qp // SLC_BLOCK
        valid = blk_ids[None, :] <= cur[:, None]
        forced = ((blk_ids[None, :] == 0) | (blk_ids[None, :] == cur[:, None])
                  | (blk_ids[None, :] == cur[:, None] - 1))
        score = jnp.where(valid, jnp.where(forced, jnp.inf, imp_s), -jnp.inf)
        _, sel = lax.top_k(score, n_sel)
        kg = ks_b[b_idx, sel]
        vg = vs_b[b_idx, sel]
        tok = sel[..., None] * SLC_BLOCK + jnp.arange(SLC_BLOCK)
        m_s = (tok <= qp[None, :, None, None]).reshape(B, 1, nq, n_sel * SLC_BLOCK)
        s_s = jnp.einsum('bqhd,bqnkd->bhqnk', qb, kg).astype(jnp.float32)
        s_s = s_s.reshape(B, H, nq, n_sel * SLC_BLOCK) * scale
        p_s = masked_softmax(s_s, m_s).reshape(B, H, nq, n_sel, SLC_BLOCK)
        o_s = jnp.einsum('bhqnk,bqnkd->bqhd', p_s.astype(vg.dtype), vg)
        start = qp[0] - pos0
        kw = lax.dynamic_slice_in_dim(wk_ext, start, nq + WINDOW, axis=1)
        vw = lax.dynamic_slice_in_dim(wv_ext, start, nq + WINDOW, axis=1)
        kpos = qp[0] - WINDOW + jnp.arange(nq + WINDOW)
        m_w = ((kpos[None, :] <= qp[:, None]) & (kpos[None, :] >= qp[:, None] - WINDOW)
               & (kpos[None, :] >= 0))
        s_w = jnp.einsum('bqhd,bkd->bhqk', qb, kw).astype(jnp.float32) * scale
        p_w = masked_softmax(s_w, m_w)
        o_w = jnp.einsum('bhqk,bkd->bqhd', p_w.astype(vw.dtype), vw)
        o = gb[..., 0:1] * o_c + gb[..., 1:2] * o_s + gb[..., 2:3] * o_w
        return o.reshape(B, nq, H * dk)

    return sweep_query_blocks(block, q, gates, pos[None])


def mla_attention(q_nope, q_rope, c_kv, k_rope, w_uk, w_uv, pos):
    B, T, H, _ = q_nope.shape
    L = c_kv.shape[1]
    scale = (MLA_NOPE + MLA_ROPE) ** -0.5
    q_lat = jnp.einsum('bthn,chn->bthc', q_nope, w_uk)
    kpos = jnp.arange(L)

    def block(ql, qr, pb):
        qp = pb[0]
        s = (jnp.einsum('bqhc,bkc->bhqk', ql, c_kv)
             + jnp.einsum('bqhr,bkr->bhqk', qr, k_rope)).astype(jnp.float32) * scale
        p = masked_softmax(s, kpos[None, :] <= qp[:, None])
        o_lat = jnp.einsum('bhqk,bkc->bqhc', p.astype(c_kv.dtype), c_kv)
        return jnp.einsum('bqhc,chv->bqhv', o_lat, w_uv).reshape(B, qp.shape[0], H * MLA_V)

    return sweep_query_blocks(block, q_lat, q_rope, pos[None])


def fox_attention(q, k, v, logf, pos):
    B, T, Hq, d = q.shape
    L, Hkv = k.shape[1], k.shape[2]
    G = Hq // Hkv
    scale = d ** -0.5
    F = jnp.cumsum(logf.astype(jnp.float32), axis=1).reshape(B, L, Hkv, G)
    Ft = jnp.transpose(F, (0, 2, 3, 1))
    Fq = F[:, L - T:]
    kpos = jnp.arange(L)

    def block(qb, fb, pb):
        qp = pb[0]
        nq = qp.shape[0]
        qg = qb.reshape(B, nq, Hkv, G, d)
        s = jnp.einsum('bqhgd,bkhd->bhgqk', qg, k).astype(jnp.float32) * scale
        bias = jnp.transpose(fb, (0, 2, 3, 1))[..., None] - Ft[:, :, :, None, :]
        p = masked_softmax(s + bias, kpos[None, :] <= qp[:, None])
        o = jnp.einsum('bhgqk,bkhd->bqhgd', p.astype(v.dtype), v)
        return o.reshape(B, nq, Hq * d)

    return sweep_query_blocks(block, q, Fq, pos[None])


def dsa_attention(q, q_idx, w_idx, k, v, k_idx, pos):
    B, T, H, d = q.shape
    L = k.shape[1]
    topk = min(DSA_TOPK, L // 4)
    scale = d ** -0.5
    kpos = jnp.arange(L)
    b_idx = jnp.arange(B)[:, None, None]

    def block(qb, qib, wib, pb):
        qp = pb[0]
        nq = qp.shape[0]
        rel = jax.nn.relu(jnp.einsum('bqhd,bkd->bqhk', qib, k_idx).astype(jnp.float32) * IDX_DK ** -0.5)
        score = jnp.einsum('bqh,bqhk->bqk', wib.astype(jnp.float32) * IDX_HEADS ** -0.5, rel)
        score = jnp.where(kpos[None, None, :] <= qp[None, :, None], score, -jnp.inf)
        _, sel = lax.top_k(score, topk)
        kg = k[b_idx, sel]
        vg = v[b_idx, sel]
        s = jnp.einsum('bqhd,bqnd->bhqn', qb, kg).astype(jnp.float32) * scale
        p = masked_softmax(s, (sel <= qp[None, :, None])[:, None])
        o = jnp.einsum('bhqn,bqnd->bqhd', p.astype(vg.dtype), vg)
        return o.reshape(B, nq, H * d)

    return sweep_query_blocks(block, q, q_idx, w_idx, pos[None])


def token_mixers(h, pos0, past, lw):
    B, T, _ = h.shape
    pos = pos0 + jnp.arange(T, dtype=jnp.int32)
    split_at = [int(i) for i in np.cumsum(IN_SPLITS)[:-1]]
    (nq, nkc, nvc, nks, nvs, nkw, nvw, ng,
     mqn, mqr, mc, mkr,
     fq, fk, fv, ff,
     dq, dk_, dv, iq, ik, iw, mg) = jnp.split(h @ lw['w_in'], split_at, axis=-1)
    rows = {
        'nsa': jnp.concatenate([rope(nkc, pos), nvc, rope(nks, pos), nvs], axis=-1),
        'mla': jnp.concatenate([rms_norm(mc, lw['mla_g_ckv']), rope(mkr, pos)], axis=-1),
        'fox': jnp.concatenate([fk, fv], axis=-1),
        'fox_logf': jax.nn.log_sigmoid(ff.astype(jnp.float32) + lw['b_fox_f']),
        'dsa': jnp.concatenate([rope(dk_, pos), dv, rope(ik, pos)], axis=-1),
    }
    full = {n: (r if past is None else jnp.concatenate([past[n], r], axis=1)) for n, r in rows.items()}
    win_rows = jnp.concatenate([rope(nkw, pos), nvw], axis=-1)
    win_prev = win_rows[:, :0] if past is None else past['win']
    win_all = jnp.concatenate([win_prev, win_rows], axis=1)
    win_ext = jnp.pad(win_all, ((0, 0), (WINDOW + T - win_all.shape[1], 0), (0, 0)))
    rows['win'] = win_all[:, -min(WINDOW, pos0 + T):]
    L = full['nsa'].shape[1]

    dn = NSA_DK
    fa = full['nsa']
    o_nsa = nsa_attention(
        rope(nq.reshape(B, T, NSA_HEADS, NSA_DK), pos),
        jax.nn.sigmoid(ng.astype(jnp.float32)).astype(h.dtype).reshape(B, T, NSA_HEADS, 3),
        pos, fa[..., :dn], fa[..., dn:2 * dn], fa[..., 2 * dn:3 * dn], fa[..., 3 * dn:],
        win_ext, pos0, lw)
    fm = full['mla']
    o_mla = mla_attention(
        mqn.reshape(B, T, MLA_HEADS, MLA_NOPE),
        rope(mqr.reshape(B, T, MLA_HEADS, MLA_ROPE), pos),
        fm[..., :MLA_KV_RANK], fm[..., MLA_KV_RANK:], lw['mla_w_uk'], lw['mla_w_uv'], pos)
    kvw = FOX_KV_HEADS * FOX_DK
    fx = full['fox']
    o_fox = fox_attention(
        fq.reshape(B, T, FOX_HEADS, FOX_DK),
        fx[..., :kvw].reshape(B, L, FOX_KV_HEADS, FOX_DK),
        fx[..., kvw:].reshape(B, L, FOX_KV_HEADS, FOX_DK),
        full['fox_logf'], pos)
    fd = full['dsa']
    o_dsa = dsa_attention(
        rope(dq.reshape(B, T, DSA_HEADS, DSA_DK), pos),
        rope(iq.reshape(B, T, IDX_HEADS, IDX_DK), pos), iw,
        fd[..., :DSA_DK], fd[..., DSA_DK:2 * DSA_DK], fd[..., 2 * DSA_DK:], pos)

    gates = jax.nn.sigmoid(mg.astype(jnp.float32)).astype(h.dtype).reshape(B, T, N_BRANCH, D_MODEL)
    branch_outs = (o_nsa, o_mla, o_fox, o_dsa)
    merged = gates[:, :, 0] * (branch_outs[0] @ lw['w_branch'][0])
    for m in range(1, N_BRANCH):
        merged = merged + gates[:, :, m] * (branch_outs[m] @ lw['w_branch'][m])
    return merged @ lw['w_out'], rows


def trunk_layer(x, pos0, past, lw):
    mixed, rows = token_mixers(rms_norm(x, lw['g_mix']), pos0, past, lw)
    x = x + mixed
    h = rms_norm(x, lw['g_mlp'])
    x = x + jnp.square(jax.nn.relu(h @ lw['w_up'])) @ lw['w_down']
    return x, rows


def setup_inputs(seed: int = 0) -> dict:
    key = jax.random.key(seed)
    keys = iter(jax.random.split(key, 40))

    def nrm(shape, scale):
        return jax.random.normal(next(keys), shape, jnp.float32) * scale

    n_pages = PAST_LEN // PAGE_SIZE
    pool = (DEC_BATCH * n_pages * 5) // 4
    win_buf = min(WINDOW, PAST_LEN)
    d_in = sum(IN_SPLITS)
    page_table = jax.random.permutation(next(keys), pool)[:DEC_BATCH * n_pages]
    page_table = page_table.reshape(DEC_BATCH, n_pages).astype(jnp.int32)
    return {
        'x_prompt': nrm((BATCH, SEQ, D_MODEL), 1.0),
        'x_sample': nrm((DEC_BATCH, DEC_SEQ, D_MODEL), 1.0),
        'cache_nsa': nrm((DEPTH, pool, PAGE_SIZE, 4 * NSA_DK), 1.0),
        'cache_mla': nrm((DEPTH, pool, PAGE_SIZE, MLA_KV_RANK + MLA_ROPE), 1.0),
        'cache_fox': nrm((DEPTH, pool, PAGE_SIZE, 2 * FOX_KV_HEADS * FOX_DK), 1.0),
        'cache_fox_logf': jax.nn.log_sigmoid(2.0 + nrm((DEPTH, pool, PAGE_SIZE, FOX_HEADS), 1.0)),
        'cache_dsa': nrm((DEPTH, pool, PAGE_SIZE, 2 * DSA_DK + IDX_DK), 1.0),
        'state_nsa_win': nrm((DEPTH, DEC_BATCH, win_buf, 2 * NSA_DK), 1.0),
        'page_table': page_table,
        'g_mix': 1.0 + nrm((DEPTH, D_MODEL), 0.02),
        'w_in': nrm((DEPTH, D_MODEL, d_in), D_MODEL ** -0.5),
        'b_fox_f': 2.0 + nrm((DEPTH, FOX_HEADS), 0.1),
        'nsa_pe_k': nrm((DEPTH, CMP_BLOCK, NSA_DK), 0.02),
        'nsa_pe_v': nrm((DEPTH, CMP_BLOCK, NSA_DK), 0.02),
        'nsa_wk1': nrm((DEPTH, CMP_BLOCK * NSA_DK, CMP_HIDDEN), (CMP_BLOCK * NSA_DK) ** -0.5),
        'nsa_wk2': nrm((DEPTH, CMP_HIDDEN, NSA_DK), CMP_HIDDEN ** -0.5),
        'nsa_wv1': nrm((DEPTH, CMP_BLOCK * NSA_DK, CMP_HIDDEN), (CMP_BLOCK * NSA_DK) ** -0.5),
        'nsa_wv2': nrm((DEPTH, CMP_HIDDEN, NSA_DK), CMP_HIDDEN ** -0.5),
        'mla_g_ckv': 1.0 + nrm((DEPTH, MLA_KV_RANK), 0.02),
        'mla_w_uk': nrm((DEPTH, MLA_KV_RANK, MLA_HEADS, MLA_NOPE), MLA_KV_RANK ** -0.5),
        'mla_w_uv': nrm((DEPTH, MLA_KV_RANK, MLA_HEADS, MLA_V), MLA_KV_RANK ** -0.5),
        'w_branch': nrm((DEPTH, N_BRANCH, BRANCH_WIDTH, D_MODEL), BRANCH_WIDTH ** -0.5),
        'w_out': nrm((DEPTH, D_MODEL, D_MODEL), D_MODEL ** -0.5),
        'g_mlp': 1.0 + nrm((DEPTH, D_MODEL), 0.02),
        'w_up': nrm((DEPTH, D_MODEL, D_FF), D_MODEL ** -0.5),
        'w_down': nrm((DEPTH, D_FF, D_MODEL), D_FF ** -0.5),
        'g_final': 1.0 + nrm((D_MODEL,), 0.02),
    }


def reference(x_prompt, x_sample, cache_nsa, cache_mla, cache_fox, cache_fox_logf, cache_dsa,
              state_nsa_win, page_table, g_mix, w_in, b_fox_f, nsa_pe_k, nsa_pe_v, nsa_wk1,
              nsa_wk2, nsa_wv1, nsa_wv2, mla_g_ckv, mla_w_uk, mla_w_uv, w_branch, w_out,
              g_mlp, w_up, w_down, g_final):
    past_len = page_table.shape[1] * cache_nsa.shape[2]
    names = ('nsa', 'mla', 'fox', 'fox_logf', 'dsa', 'win')
    new_p = {n: [] for n in names}
    new_s = {n: [] for n in names}
    xp, xs = x_prompt, x_sample
    for l in range(DEPTH):
        lw = {
            'g_mix': g_mix[l], 'w_in': w_in[l], 'b_fox_f': b_fox_f[l],
            'nsa_pe_k': nsa_pe_k[l], 'nsa_pe_v': nsa_pe_v[l],
            'nsa_wk1': nsa_wk1[l], 'nsa_wk2': nsa_wk2[l], 'nsa_wv1': nsa_wv1[l], 'nsa_wv2': nsa_wv2[l],
            'mla_g_ckv': mla_g_ckv[l], 'mla_w_uk': mla_w_uk[l], 'mla_w_uv': mla_w_uv[l],
            'w_branch': w_branch[l], 'w_out': w_out[l],
            'g_mlp': g_mlp[l], 'w_up': w_up[l], 'w_down': w_down[l],
        }
        past = {
            'nsa': gather_pages(cache_nsa, l, page_table),
            'mla': gather_pages(cache_mla, l, page_table),
            'fox': gather_pages(cache_fox, l, page_table),
            'fox_logf': gather_pages(cache_fox_logf, l, page_table),
            'dsa': gather_pages(cache_dsa, l, page_table),
            'win': state_nsa_win[l],
        }
        xp, rows_p = trunk_layer(xp, 0, None, lw)
        xs, rows_s = trunk_layer(xs, past_len, past, lw)
        for n in names:
            new_p[n].append(rows_p[n])
            new_s[n].append(rows_s[n])
    y_prompt = rms_norm(xp, g_final)
    y_sample = rms_norm(xs, g_final)
    return (y_prompt, y_sample,
            jnp.stack(new_p['nsa']), jnp.stack(new_p['mla']), jnp.stack(new_p['fox']),
            jnp.stack(new_p['fox_logf']), jnp.stack(new_p['dsa']), jnp.stack(new_p['win']),
            jnp.stack(new_s['nsa']), jnp.stack(new_s['mla']), jnp.stack(new_s['fox']),
            jnp.stack(new_s['fox_logf']), jnp.stack(new_s['dsa']), jnp.stack(new_s['win']))
```

```python
import functools
import math

import jax
import jax.numpy as jnp
from jax import lax
from jax.experimental import pallas as pl
from jax.experimental.pallas import tpu as pltpu

F32 = jnp.float32
BF16 = jnp.bfloat16
I32 = jnp.int32

ROPE_THETA = 10000.0
EPS = 1e-6
N_BRANCH = 4
HEADS = 4
DK = 64
CMP_BLOCK = 32
CMP_STRIDE = 16
SLC_BLOCK = 64
SLC_TOPN = 16
WINDOW = 512
MLA_KV_RANK = 256
MLA_NOPE = 64
MLA_ROPE = 32
FOX_KV_HEADS = 2
DSA_TOPK = 256
IDX_HEADS = 4

LANES = 128
NEG = -1e30
BIG = 3e38
SOFTMAX_FLOOR = 1e-30
VMEM_LIMIT = 56 * 1024 * 1024


def _round_up(x, m):
    return -(-x // m) * m


def _pick_tile(n, cap):
    best = LANES
    for t in range(LANES, cap + 1, LANES):
        if n % t == 0:
            best = t
    return best


def _params(*sem):
    return pltpu.CompilerParams(dimension_semantics=sem, vmem_limit_bytes=VMEM_LIMIT)


def _dot(a, b):
    return jnp.dot(a.astype(BF16), b.astype(BF16), preferred_element_type=F32)


def _dot_nt(a, b):
    return lax.dot_general(a.astype(BF16), b.astype(BF16), (((1,), (1,)), ((), ())),
                           preferred_element_type=F32)


def _dot_split3(a, b01):
    hi = a.astype(BF16)
    r1 = a - hi.astype(F32)
    mid = r1.astype(BF16)
    lo = (r1 - mid.astype(F32)).astype(BF16)
    out = jnp.dot(hi, b01, preferred_element_type=F32)
    out = out + jnp.dot(mid, b01, preferred_element_type=F32)
    return out + jnp.dot(lo, b01, preferred_element_type=F32)


def _stack_heads(x, width):
    return jnp.concatenate([x[:, h * width:(h + 1) * width] for h in range(HEADS)], axis=0)


def _tile_rows(x, n):
    return jnp.concatenate([x] * n, axis=0)


def _softmax_step(s, mask, v, carry):
    m, l, acc = carry
    sm = jnp.where(mask, s, NEG)
    m_new = jnp.maximum(m, jnp.max(sm, axis=-1, keepdims=True))
    p = jnp.where(mask, jnp.exp(sm - m_new), 0.0)
    alpha = jnp.exp(m - m_new)
    l = alpha * l + jnp.sum(p, axis=-1, keepdims=True)
    acc = alpha * acc + _dot(p, v)
    return m_new, l, acc


def _softmax_init(rows, dv):
    return (jnp.full((rows, 1), NEG, F32), jnp.zeros((rows, 1), F32), jnp.zeros((rows, dv), F32))


def _softmax_done(carry):
    _, l, acc = carry
    return acc / jnp.maximum(l, SOFTMAX_FLOOR)


def _count(pred):
    return jnp.sum(jnp.where(pred, 1, 0).astype(I32), axis=-1, keepdims=True)


def _select_topk(score, cand, k, n_bisect=24):
    rows, n = score.shape
    sc = jnp.where(cand, score, -BIG)
    ncand = _count(cand)
    small = ncand <= k
    lo0 = jnp.min(jnp.where(cand, score, BIG), axis=-1, keepdims=True)
    top = jnp.max(sc, axis=-1, keepdims=True)
    done0 = jnp.where(small, 1.0, jnp.where(_count(sc >= top) >= k, 1.0, 0.0))

    def bisect(_, c):
        lo, hi = c
        mid = 0.5 * (lo + hi)
        ge = _count(sc >= mid) >= k
        return jnp.where(ge, mid, lo), jnp.where(ge, hi, mid)

    _, hi = lax.fori_loop(0, n_bisect, bisect, (lo0, top))

    def pending(c):
        return jnp.min(c[2]) < 0.5

    def step_down(c):
        hi, t, done = c
        nxt = jnp.max(jnp.where(sc < hi, sc, -BIG), axis=-1, keepdims=True)
        fin = _count(sc >= nxt) >= k
        is_done = done > 0.5
        t = jnp.where(is_done, t, nxt)
        hi = jnp.where(is_done, hi, nxt)
        done = jnp.where(fin, 1.0, done)
        return hi, t, done

    _, t, _ = lax.while_loop(pending, step_down, (hi, top, done0))

    above = sc > t
    need = k - _count(above)
    idx = lax.broadcasted_iota(I32, (rows, n), 1)
    tie_idx = jnp.where(sc == t, idx, jnp.int32(1 << 30))
    idx_bits = max(1, (n - 1).bit_length())

    def idx_search(i, j):
        c = j | jnp.left_shift(jnp.int32(1), idx_bits - 1 - i)
        return jnp.where(_count(tie_idx < c) < need, c, j)

    j = lax.fori_loop(0, idx_bits, idx_search, jnp.zeros((rows, 1), I32))
    chosen = jnp.where(above, 1.0, jnp.where(tie_idx <= j, 1.0, 0.0))
    return jnp.where(small, jnp.where(cand, 1.0, 0.0), chosen)


def _norm_matmul_kernel(x_ref, g_ref, w_ref, o_ref, h_ref):
    @pl.when(pl.program_id(1) == 0)
    def _():
        x = x_ref[...]
        y = x * lax.rsqrt(jnp.mean(x * x, axis=-1, keepdims=True) + EPS)
        h_ref[...] = (y * g_ref[...]).astype(BF16)

    o_ref[...] = jnp.dot(h_ref[...], w_ref[...], preferred_element_type=F32)


def _norm_matmul(x, g, w):
    m, d = x.shape
    n = w.shape[1]
    tm = math.gcd(m, 1024)
    tn = _pick_tile(n, 512)
    return pl.pallas_call(
        _norm_matmul_kernel,
        grid=(m // tm, n // tn),
        in_specs=[pl.BlockSpec((tm, d), lambda i, j: (i, 0)),
                  pl.BlockSpec((1, d), lambda i, j: (0, 0)),
                  pl.BlockSpec((d, tn), lambda i, j: (0, j))],
        out_specs=pl.BlockSpec((tm, tn), lambda i, j: (i, j)),
        out_shape=jax.ShapeDtypeStruct((m, n), F32),
        scratch_shapes=[pltpu.VMEM((tm, d), BF16)],
        compiler_params=_params("parallel", "arbitrary"),
        name="norm_matmul",
    )(x, g.reshape(1, d), w)


def _merge_kernel(x_ref, o0_ref, o1_ref, o2_ref, o3_ref, mg_ref, wb_ref, wo_ref, y_ref):
    d = x_ref.shape[1]
    merged = None
    for m, o_ref in enumerate((o0_ref, o1_ref, o2_ref, o3_ref)):
        gate = jax.nn.sigmoid(mg_ref[:, m * d:(m + 1) * d])
        term = gate * _dot(o_ref[...], wb_ref[m])
        merged = term if merged is None else merged + term
    y_ref[...] = x_ref[...] + _dot(merged, wo_ref[...])


def _merge(x, outs, mg, wb, wo):
    m, d = x.shape
    bw = outs[0].shape[1]
    tm = math.gcd(m, 256)
    row = lambda i: (i, 0)
    return pl.pallas_call(
        _merge_kernel,
        grid=(m // tm,),
        in_specs=[pl.BlockSpec((tm, d), row)] + [pl.BlockSpec((tm, bw), row)] * N_BRANCH
        + [pl.BlockSpec((tm, N_BRANCH * d), row),
           pl.BlockSpec((N_BRANCH, bw, d), lambda i: (0, 0, 0)),
           pl.BlockSpec((d, d), lambda i: (0, 0))],
        out_specs=pl.BlockSpec((tm, d), row),
        out_shape=jax.ShapeDtypeStruct((m, d), F32),
        compiler_params=_params("parallel"),
        name="merge",
    )(x, *outs, mg, wb, wo)


def _mlp_kernel(x_ref, g_ref, wu_ref, wd_ref, y_ref, h_ref, acc_ref):
    f = pl.program_id(1)

    @pl.when(f == 0)
    def _():
        x = x_ref[...]
        y = x * lax.rsqrt(jnp.mean(x * x, axis=-1, keepdims=True) + EPS)
        h_ref[...] = (y * g_ref[...]).astype(BF16)
        acc_ref[...] = x

    u = jnp.dot(h_ref[...], wu_ref[...], preferred_element_type=F32)
    acc_ref[...] += _dot(jnp.square(jnp.maximum(u, 0.0)), wd_ref[...])

    @pl.when(f == pl.num_programs(1) - 1)
    def _():
        y_ref[...] = acc_ref[...]


def _mlp(x, g, wu, wd):
    m, d = x.shape
    dff = wu.shape[1]
    tm = math.gcd(m, 1024)
    tf = _pick_tile(dff, 512)
    return pl.pallas_call(
        _mlp_kernel,
        grid=(m // tm, dff // tf),
        in_specs=[pl.BlockSpec((tm, d), lambda i, f: (i, 0)),
                  pl.BlockSpec((1, d), lambda i, f: (0, 0)),
                  pl.BlockSpec((d, tf), lambda i, f: (0, f)),
                  pl.BlockSpec((tf, d), lambda i, f: (f, 0))],
        out_specs=pl.BlockSpec((tm, d), lambda i, f: (i, 0)),
        out_shape=jax.ShapeDtypeStruct((m, d), F32),
        scratch_shapes=[pltpu.VMEM((tm, d), BF16), pltpu.VMEM((tm, d), F32)],
        compiler_params=_params("parallel", "arbitrary"),
        name="mlp",
    )(x, g.reshape(1, d), wu, wd)


def _rms_kernel(x_ref, g_ref, y_ref):
    x = x_ref[...]
    y_ref[...] = x * lax.rsqrt(jnp.mean(x * x, axis=-1, keepdims=True) + EPS) * g_ref[...]


def _rms_norm_rows(x, g):
    m, d = x.shape
    tm = math.gcd(m, 1024)
    return pl.pallas_call(
        _rms_kernel,
        grid=(m // tm,),
        in_specs=[pl.BlockSpec((tm, d), lambda i: (i, 0)), pl.BlockSpec((1, d), lambda i: (0, 0))],
        out_specs=pl.BlockSpec((tm, d), lambda i: (i, 0)),
        out_shape=jax.ShapeDtypeStruct((m, d), F32),
        compiler_params=_params("parallel"),
        name="final_norm",
    )(x, g.reshape(1, d))


def _gather_kernel(pt_ref, cache_ref, new_ref, out_ref, tail_ref, sem, tail_sem, *,
                   layer, n_pages, page, t_new, lane_major):
    b = pl.program_id(0)
    tail_ref[...] = jnp.zeros_like(tail_ref)
    if lane_major:
        tail_ref[:, 0:t_new] = new_ref[0]
        dst = lambda p: out_ref.at[b, :, pl.ds(p * page, page)]
    else:
        tail_ref[0:t_new, :] = new_ref[0]
        dst = lambda p: out_ref.at[b, pl.ds(p * page, page)]
    tail_copy = pltpu.make_async_copy(tail_ref, dst(n_pages), tail_sem)
    tail_copy.start()

    def page_copy(p):
        return pltpu.make_async_copy(cache_ref.at[layer, pt_ref[b, p]], dst(p), sem)

    for p in range(n_pages):
        page_copy(p).start()
    for p in range(n_pages):
        page_copy(p).wait()
    tail_copy.wait()


def _gather_rows(cache, layer, page_table, new, lane_major=False):
    bsz, n_pages = page_table.shape
    if lane_major:
        w, page = cache.shape[2], cache.shape[3]
        t_new = new.shape[2]
        out_shape = (bsz, w, (n_pages + 1) * page)
        tail_shape = (w, page)
        new_block = (1, w, t_new)
    else:
        page, w = cache.shape[2], cache.shape[3]
        t_new = new.shape[1]
        out_shape = (bsz, (n_pages + 1) * page, w)
        tail_shape = (page, w)
        new_block = (1, t_new, w)
    kern = functools.partial(_gather_kernel, layer=layer, n_pages=n_pages, page=page,
                             t_new=t_new, lane_major=lane_major)
    return pl.pallas_call(
        kern,
        grid_spec=pltpu.PrefetchScalarGridSpec(
            num_scalar_prefetch=1,
            grid=(bsz,),
            in_specs=[pl.BlockSpec(memory_space=pl.ANY),
                      pl.BlockSpec(new_block, lambda b, pt: (b, 0, 0))],
            out_specs=pl.BlockSpec(memory_space=pl.ANY),
            scratch_shapes=[pltpu.VMEM(tail_shape, F32), pltpu.SemaphoreType.DMA,
                            pltpu.SemaphoreType.DMA]),
        out_shape=jax.ShapeDtypeStruct(out_shape, F32),
        compiler_params=_params("arbitrary"),
        name="gather_rows",
    )(page_table, cache, new)


def _cumsum_kernel(x_ref, f_ref):
    n_tiles = x_ref.shape[2] // LANES
    r = lax.broadcasted_iota(I32, (LANES, LANES), 0)
    c = lax.broadcasted_iota(I32, (LANES, LANES), 1)
    upper = jnp.where(r <= c, 1.0, 0.0).astype(BF16)

    def body(j, carry):
        sl = pl.ds(pl.multiple_of(j * LANES, LANES), LANES)
        cs = _dot_split3(x_ref[0, :, sl], upper) + carry
        f_ref[0, :, sl] = cs
        return cs[:, LANES - 1:LANES]

    lax.fori_loop(0, n_tiles, body, jnp.zeros((x_ref.shape[1], 1), F32))


def _cumsum_lanes(x):
    bsz, rows, n = x.shape
    return pl.pallas_call(
        _cumsum_kernel,
        grid=(bsz,),
        in_specs=[pl.BlockSpec((1, rows, n), lambda b: (b, 0, 0))],
        out_specs=pl.BlockSpec((1, rows, n), lambda b: (b, 0, 0)),
        out_shape=jax.ShapeDtypeStruct((bsz, rows, n), F32),
        compiler_params=_params("parallel"),
        name="fox_cumsum",
    )(x)


def _fox_kernel(q_ref, fq_ref, kv_ref, ft_ref, o_ref, *, tq, tk, pos0):
    qstart = pos0 + pl.program_id(1) * tq
    n_tiles = (qstart + tq - 1) // tk + 1
    group = HEADS // FOX_KV_HEADS
    kvw = FOX_KV_HEADS * DK
    scale = DK ** -0.5
    q = q_ref[0]
    fq = fq_ref[0]
    qpos = _tile_rows(qstart + lax.broadcasted_iota(I32, (tq, 1), 0), group)
    for hk in range(FOX_KV_HEADS):
        heads = range(hk * group, (hk + 1) * group)
        qs = jnp.concatenate([q[:, h * DK:(h + 1) * DK] for h in heads], axis=0).astype(BF16)
        fqs = jnp.concatenate([fq[:, h:h + 1] for h in heads], axis=0)

        def body(j, carry, hk=hk, heads=heads, qs=qs, fqs=fqs):
            sl = pl.ds(pl.multiple_of(j * tk, tk), tk)
            k = kv_ref[0, sl, hk * DK:(hk + 1) * DK]
            v = kv_ref[0, sl, kvw + hk * DK:kvw + (hk + 1) * DK]
            ft = ft_ref[0, :, sl]
            fts = jnp.concatenate([jnp.broadcast_to(ft[h:h + 1, :], (tq, tk)) for h in heads], axis=0)
            s = _dot_nt(qs, k) * scale + (fqs - fts)
            kpos = j * tk + lax.broadcasted_iota(I32, (1, tk), 1)
            return _softmax_step(s, kpos <= qpos, v, carry)

        o = _softmax_done(lax.fori_loop(0, n_tiles, body, _softmax_init(group * tq, DK)))
        for g, h in enumerate(heads):
            o_ref[0, :, h * DK:(h + 1) * DK] = o[g * tq:(g + 1) * tq]


def _fox_attention(q, fq, kv, ft, pos0):
    bsz, t, _ = q.shape
    lp = kv.shape[1]
    tq = math.gcd(t, LANES)
    tk = _pick_tile(lp, 768)
    kern = functools.partial(_fox_kernel, tq=tq, tk=tk, pos0=pos0)
    return pl.pallas_call(
        kern,
        grid=(bsz, t // tq),
        in_specs=[pl.BlockSpec((1, tq, HEADS * DK), lambda b, i: (b, i, 0)),
                  pl.BlockSpec((1, tq, HEADS), lambda b, i: (b, i, 0)),
                  pl.BlockSpec((1, lp, kv.shape[2]), lambda b, i: (b, 0, 0)),
                  pl.BlockSpec((1, HEADS, lp), lambda b, i: (b, 0, 0))],
        out_specs=pl.BlockSpec((1, tq, HEADS * DK), lambda b, i: (b, i, 0)),
        out_shape=jax.ShapeDtypeStruct((bsz, t, HEADS * DK), F32),
        compiler_params=_params("parallel", "arbitrary"),
        name="fox_attention",
    )(q, fq, kv, ft)


def _mla_kernel(qn_ref, qr_ref, kv_ref, wuk_ref, wuv_ref, o_ref, *, tq, tk, pos0):
    qstart = pos0 + pl.program_id(1) * tq
    n_tiles = (qstart + tq - 1) // tk + 1
    scale = (MLA_NOPE + MLA_ROPE) ** -0.5
    qn = qn_ref[0]
    qr = qr_ref[0]
    parts = []
    for h in range(HEADS):
        q_lat = _dot(qn[:, h * MLA_NOPE:(h + 1) * MLA_NOPE], wuk_ref[h])
        parts.append(jnp.concatenate([q_lat, qr[:, h * MLA_ROPE:(h + 1) * MLA_ROPE]], axis=-1))
    qs = jnp.concatenate(parts, axis=0).astype(BF16)
    qpos = _tile_rows(qstart + lax.broadcasted_iota(I32, (tq, 1), 0), HEADS)

    def body(j, carry):
        kt = kv_ref[0, pl.ds(pl.multiple_of(j * tk, tk), tk), :].astype(BF16)
        s = _dot_nt(qs, kt) * scale
        kpos = j * tk + lax.broadcasted_iota(I32, (1, tk), 1)
        return _softmax_step(s, kpos <= qpos, kt[:, :MLA_KV_RANK], carry)

    o_lat = _softmax_done(lax.fori_loop(0, n_tiles, body, _softmax_init(HEADS * tq, MLA_KV_RANK)))
    for h in range(HEADS):
        o_ref[0, :, h * DK:(h + 1) * DK] = _dot(o_lat[h * tq:(h + 1) * tq], wuv_ref[h])


def _mla_attention(qn, qr, kv, wuk, wuv, pos0):
    bsz, t, _ = qn.shape
    lp = kv.shape[1]
    tq = math.gcd(t, LANES)
    tk = _pick_tile(lp, 768)
    kern = functools.partial(_mla_kernel, tq=tq, tk=tk, pos0=pos0)
    return pl.pallas_call(
        kern,
        grid=(bsz, t // tq),
        in_specs=[pl.BlockSpec((1, tq, qn.shape[2]), lambda b, i: (b, i, 0)),
                  pl.BlockSpec((1, tq, qr.shape[2]), lambda b, i: (b, i, 0)),
                  pl.BlockSpec((1, lp, kv.shape[2]), lambda b, i: (b, 0, 0)),
                  pl.BlockSpec(wuk.shape, lambda b, i: (0, 0, 0)),
                  pl.BlockSpec(wuv.shape, lambda b, i: (0, 0, 0))],
        out_specs=pl.BlockSpec((1, tq, HEADS * DK), lambda b, i: (b, i, 0)),
        out_shape=jax.ShapeDtypeStruct((bsz, t, HEADS * DK), F32),
        compiler_params=_params("parallel", "arbitrary"),
        name="mla_attention",
    )(qn, qr, kv, wuk, wuv)


def _dsa_kernel(q_ref, qi_ref, wi_ref, kv_ref, o_ref, sel_ref, *, tq, tk, pos0, topk):
    lp = kv_ref.shape[1]
    qstart = pos0 + pl.program_id(1) * tq
    n_tiles = (qstart + tq - 1) // tk + 1
    qpos1 = qstart + lax.broadcasted_iota(I32, (tq, 1), 0)
    qpos = _tile_rows(qpos1, HEADS)

    qis = _stack_heads(qi_ref[0], DK).astype(BF16)
    wi = wi_ref[0] * IDX_HEADS ** -0.5
    tiles = []
    for j in range(lp // tk):
        kidx = kv_ref[0, j * tk:(j + 1) * tk, 2 * DK:3 * DK]
        rel = jnp.maximum(_dot_nt(qis, kidx) * DK ** -0.5, 0.0)
        sc = wi[:, 0:1] * rel[0:tq]
        for h in range(1, IDX_HEADS):
            sc = sc + wi[:, h:h + 1] * rel[h * tq:(h + 1) * tq]
        tiles.append(sc)
    score = jnp.concatenate(tiles, axis=-1)
    causal = lax.broadcasted_iota(I32, (1, lp), 1) <= qpos1
    chosen = _select_topk(score, causal, jnp.full((tq, 1), topk, I32))
    sel_ref[...] = chosen

    qs = _stack_heads(q_ref[0], DK).astype(BF16)
    scale = DK ** -0.5

    def body(j, carry):
        sl = pl.ds(pl.multiple_of(j * tk, tk), tk)
        k = kv_ref[0, sl, 0:DK]
        v = kv_ref[0, sl, DK:2 * DK]
        s = _dot_nt(qs, k) * scale
        kpos = j * tk + lax.broadcasted_iota(I32, (1, tk), 1)
        mask = jnp.where(kpos <= qpos, _tile_rows(sel_ref[:, sl], HEADS), 0.0) > 0.5
        return _softmax_step(s, mask, v, carry)

    o = _softmax_done(lax.fori_loop(0, n_tiles, body, _softmax_init(HEADS * tq, DK)))
    for h in range(HEADS):
        o_ref[0, :, h * DK:(h + 1) * DK] = o[h * tq:(h + 1) * tq]


def _dsa_attention(q, qi, wi, kv, pos0, topk):
    bsz, t, _ = q.shape
    lp = kv.shape[1]
    tq = math.gcd(t, LANES)
    tk = _pick_tile(lp, 768)
    kern = functools.partial(_dsa_kernel, tq=tq, tk=tk, pos0=pos0, topk=topk)
    return pl.pallas_call(
        kern,
        grid=(bsz, t // tq),
        in_specs=[pl.BlockSpec((1, tq, HEADS * DK), lambda b, i: (b, i, 0)),
                  pl.BlockSpec((1, tq, HEADS * DK), lambda b, i: (b, i, 0)),
                  pl.BlockSpec((1, tq, IDX_HEADS), lambda b, i: (b, i, 0)),
                  pl.BlockSpec((1, lp, kv.shape[2]), lambda b, i: (b, 0, 0))],
        out_specs=pl.BlockSpec((1, tq, HEADS * DK), lambda b, i: (b, i, 0)),
        out_shape=jax.ShapeDtypeStruct((bsz, t, HEADS * DK), F32),
        scratch_shapes=[pltpu.VMEM((tq, lp), F32)],
        compiler_params=_params("parallel", "arbitrary"),
        name="dsa_attention",
    )(q, qi, wi, kv)


def _compress_kernel(x_ref, w1_ref, pek_ref, pev_ref, wk1_ref, wv1_ref, wk2_ref, wv2_ref, o_ref,
                     *, n_ch):
    acc = jnp.zeros((n_ch, 4 * DK), F32)
    for r in range(CMP_STRIDE):
        rows = x_ref[0, pl.ds(r, n_ch, stride=CMP_STRIDE), :]
        acc = acc + _dot(rows, w1_ref[r])
    nxt = pltpu.roll(acc, n_ch - 1, 0)
    hk = acc[:, 0:DK] + nxt[:, DK:2 * DK] + jnp.sum(pek_ref[...] * wk1_ref[...], axis=0, keepdims=True)
    hv = acc[:, 2 * DK:3 * DK] + nxt[:, 3 * DK:4 * DK] + jnp.sum(pev_ref[...] * wv1_ref[...], axis=0, keepdims=True)
    o_ref[...] = jnp.zeros_like(o_ref)
    o_ref[0, 0:n_ch, 0:DK] = _dot(hk * jax.nn.sigmoid(hk), wk2_ref[...])
    o_ref[0, 0:n_ch, DK:2 * DK] = _dot(hv * jax.nn.sigmoid(hv), wv2_ref[...])


def _nsa_compress(kv, w1, pek, pev, wk1, wv1, wk2, wv2):
    bsz, lp, w = kv.shape
    n_ch = lp // CMP_STRIDE
    ncp = _round_up(n_ch, LANES)
    full = lambda a: pl.BlockSpec(a.shape, lambda b: (0,) * a.ndim)
    kern = functools.partial(_compress_kernel, n_ch=n_ch)
    return pl.pallas_call(
        kern,
        grid=(bsz,),
        in_specs=[pl.BlockSpec((1, lp, 2 * DK), lambda b: (b, 0, 0))]
        + [full(a) for a in (w1, pek, pev, wk1, wv1, wk2, wv2)],
        out_specs=pl.BlockSpec((1, ncp, 2 * DK), lambda b: (b, 0, 0)),
        out_shape=jax.ShapeDtypeStruct((bsz, ncp, 2 * DK), F32),
        compiler_params=_params("parallel"),
        name="nsa_compress",
    )(kv, w1, pek, pev, wk1, wv1, wk2, wv2)


def _nsa_kernel(q_ref, g_ref, kv_ref, cmp_ref, win_ref, o_ref, *, tq, tk, tw, pos0, wpos0, n_sel):
    ncp = cmp_ref.shape[1]
    nsp = _round_up(-(-ncp * CMP_STRIDE // SLC_BLOCK) + 1, LANES)
    qstart = pos0 + pl.program_id(1) * tq
    qpos1 = qstart + lax.broadcasted_iota(I32, (tq, 1), 0)
    qpos = _tile_rows(qpos1, HEADS)
    qs = _stack_heads(q_ref[0], DK).astype(BF16)
    scale = DK ** -0.5
    rows = HEADS * tq

    ck = cmp_ref[0, :, 0:DK]
    cv = cmp_ref[0, :, DK:2 * DK]
    s_c = _dot_nt(qs, ck) * scale
    cmp_end = lax.broadcasted_iota(I32, (1, ncp), 1) * CMP_STRIDE + (CMP_BLOCK - 1)
    m_c = cmp_end <= qpos
    sm = jnp.where(m_c, s_c, NEG)
    e = jnp.where(m_c, jnp.exp(sm - jnp.max(sm, axis=-1, keepdims=True)), 0.0)
    p_c = e / jnp.maximum(jnp.sum(e, axis=-1, keepdims=True), SOFTMAX_FLOOR)
    o_c = _dot(p_c, cv)

    imp = p_c[0:tq]
    for h in range(1, HEADS):
        imp = imp + p_c[h * tq:(h + 1) * tq]
    r_s = SLC_BLOCK // CMP_STRIDE
    r_c = CMP_BLOCK // CMP_STRIDE
    ci = lax.broadcasted_iota(I32, (ncp, nsp), 0)
    bj = lax.broadcasted_iota(I32, (ncp, nsp), 1) * r_s
    overlap = jnp.where(ci >= bj - (r_c - 1), jnp.where(ci <= bj + (r_s - 1), 1.0, 0.0), 0.0)
    imp_s = _dot_split3(imp, overlap.astype(BF16))
    blk = lax.broadcasted_iota(I32, (1, nsp), 1)
    cur = jnp.right_shift(qpos1, int(math.log2(SLC_BLOCK)))
    forced = (blk == 0) | (blk == cur) | (blk == cur - 1)
    cand = (blk <= cur) & jnp.logical_not(forced)
    chosen = _select_topk(imp_s, cand, n_sel - _count(forced))
    sel = jnp.where(forced, 1.0, chosen).astype(BF16)

    def slc_body(j, carry):
        sl = pl.ds(pl.multiple_of(j * tk, tk), tk)
        k = kv_ref[0, sl, 2 * DK:3 * DK]
        v = kv_ref[0, sl, 3 * DK:4 * DK]
        s = _dot_nt(qs, k) * scale
        kpos = j * tk + lax.broadcasted_iota(I32, (1, tk), 1)
        kblk = jnp.right_shift(j * tk + lax.broadcasted_iota(I32, (nsp, tk), 1), int(math.log2(SLC_BLOCK)))
        expand = jnp.where(lax.broadcasted_iota(I32, (nsp, tk), 0) == kblk, 1.0, 0.0).astype(BF16)
        picked = jnp.dot(sel, expand, preferred_element_type=F32)
        mask = _tile_rows(jnp.where(kpos <= qpos1, picked, 0.0), HEADS) > 0.5
        return _softmax_step(s, mask, v, carry)

    n_tiles = (qstart + tq - 1) // tk + 1
    o_s = _softmax_done(lax.fori_loop(0, n_tiles, slc_body, _softmax_init(rows, DK)))

    def win_body(j, carry):
        sl = pl.ds(pl.multiple_of(j * tw, tw), tw)
        k = win_ref[0, sl, 0:DK]
        v = win_ref[0, sl, DK:2 * DK]
        s = _dot_nt(qs, k) * scale
        kpos = wpos0 + j * tw + lax.broadcasted_iota(I32, (1, tw), 1)
        mask = (kpos <= qpos) & (kpos >= qpos - WINDOW) & (kpos >= 0)
        return _softmax_step(s, mask, v, carry)

    lo_t = jnp.maximum(qstart - WINDOW - wpos0, 0) // tw
    hi_t = (qstart + tq - 1 - wpos0) // tw + 1
    o_w = _softmax_done(lax.fori_loop(lo_t, hi_t, win_body, _softmax_init(rows, DK)))

    g = g_ref[0]
    for h in range(HEADS):
        hs = slice(h * tq, (h + 1) * tq)
        o_ref[0, :, h * DK:(h + 1) * DK] = (g[:, 3 * h:3 * h + 1] * o_c[hs] + g[:, 3 * h + 1:3 * h + 2] * o_s[hs]
                                            + g[:, 3 * h + 2:3 * h + 3] * o_w[hs])


def _nsa_attention(q, g, kv, cmp, win, pos0, wpos0, n_sel):
    bsz, t, _ = q.shape
    lp = kv.shape[1]
    tq = math.gcd(t, LANES)
    tk = _pick_tile(lp, 768)
    tw = LANES
    kern = functools.partial(_nsa_kernel, tq=tq, tk=tk, tw=tw, pos0=pos0, wpos0=wpos0, n_sel=n_sel)
    return pl.pallas_call(
        kern,
        grid=(bsz, t // tq),
        in_specs=[pl.BlockSpec((1, tq, HEADS * DK), lambda b, i: (b, i, 0)),
                  pl.BlockSpec((1, tq, g.shape[2]), lambda b, i: (b, i, 0)),
                  pl.BlockSpec((1, lp, kv.shape[2]), lambda b, i: (b, 0, 0)),
                  pl.BlockSpec((1,) + cmp.shape[1:], lambda b, i: (b, 0, 0)),
                  pl.BlockSpec((1,) + win.shape[1:], lambda b, i: (b, 0, 0))],
        out_specs=pl.BlockSpec((1, tq, HEADS * DK), lambda b, i: (b, i, 0)),
        out_shape=jax.ShapeDtypeStruct((bsz, t, HEADS * DK), F32),
        compiler_params=_params("parallel", "arbitrary"),
        name="nsa_attention",
    )(q, g, kv, cmp, win)


def _rope(x, pos):
    half = x.shape[-1] // 2
    inv = ROPE_THETA ** (-jnp.arange(half, dtype=F32) / half)
    ang = pos.astype(F32)[:, None] * inv[None, :]
    if x.ndim == 4:
        ang = ang[:, None, :]
    cos, sin = jnp.cos(ang), jnp.sin(ang)
    x1, x2 = x[..., :half], x[..., half:]
    return jnp.concatenate([x1 * cos - x2 * sin, x2 * cos + x1 * sin], axis=-1)


def _rms(x, g):
    return x * lax.rsqrt(jnp.mean(x * x, axis=-1, keepdims=True) + EPS) * g


def _in_splits(d_model):
    return (HEADS * DK, DK, DK, DK, DK, DK, DK, 3 * HEADS,
            HEADS * MLA_NOPE, HEADS * MLA_ROPE, MLA_KV_RANK, MLA_ROPE,
            HEADS * DK, FOX_KV_HEADS * DK, FOX_KV_HEADS * DK, HEADS,
            HEADS * DK, DK, DK, IDX_HEADS * DK, DK, IDX_HEADS,
            N_BRANCH * d_model)


def _token_mixers(x2, bsz, t, pos0, past, lw):
    d_model = x2.shape[1]
    splits = _in_splits(d_model)
    proj = _norm_matmul(x2, lw['g_mix'], lw['w_in'])
    cuts = [0]
    for s in splits:
        cuts.append(cuts[-1] + s)
    (nq, nkc, nvc, nks, nvs, nkw, nvw, ng, mqn, mqr, mc, mkr, fq, fk, fv, ff,
     dq, dk_, dv, iq, ik, iw, mg) = [proj[:, a:b].reshape(bsz, t, b - a) for a, b in zip(cuts[:-1], cuts[1:])]
    pos = pos0 + jnp.arange(t, dtype=I32)
    heads4 = lambda a: a.reshape(bsz, t, HEADS, -1)
    rope_heads = lambda a: _rope(heads4(a), pos).reshape(bsz, t, -1)
    rows = {
        'nsa': jnp.concatenate([_rope(nkc, pos), nvc, _rope(nks, pos), nvs], axis=-1),
        'mla': jnp.concatenate([_rms(mc, lw['mla_g_ckv']), _rope(mkr, pos)], axis=-1),
        'fox': jnp.concatenate([fk, fv], axis=-1),
        'fox_logf': jax.nn.log_sigmoid(ff + lw['b_fox_f']),
        'dsa': jnp.concatenate([_rope(dk_, pos), dv, _rope(ik, pos)], axis=-1),
    }
    win_rows = jnp.concatenate([_rope(nkw, pos), nvw], axis=-1)
    logf_t = jnp.swapaxes(rows['fox_logf'], 1, 2)
    if past is None:
        full = dict(rows)
        logf_full_t = logf_t
        win_all = win_rows
        wpos0 = 0
    else:
        pt = past['page_table']
        full = {n: _gather_rows(past[n], past['layer'], pt, rows[n]) for n in ('nsa', 'mla', 'fox', 'dsa')}
        logf_full_t = _gather_rows(past['fox_logf_t'], past['layer'], pt, logf_t, lane_major=True)
        win_all = jnp.concatenate([past['win'], win_rows], axis=1)
        wpos0 = pos0 - past['win'].shape[1]
    seq_len = pos0 + t
    rows['win'] = win_all[:, -min(WINDOW, seq_len):]
    win_pad = _round_up(win_all.shape[1], LANES) - win_all.shape[1]
    win_buf = jnp.pad(win_all, ((0, 0), (0, win_pad), (0, 0)))

    cmp = _nsa_compress(full['nsa'], lw['nsa_w1'], lw['nsa_pe_k'], lw['nsa_pe_v'], lw['nsa_wk1'],
                        lw['nsa_wv1'], lw['nsa_wk2'], lw['nsa_wv2'])
    n_s = -(-seq_len // SLC_BLOCK)
    o_nsa = _nsa_attention(rope_heads(nq), jax.nn.sigmoid(ng), full['nsa'], cmp, win_buf, pos0, wpos0,
                           min(SLC_TOPN, n_s))
    o_mla = _mla_attention(mqn, rope_heads(mqr), full['mla'], lw['mla_w_uk'], lw['mla_w_uv'], pos0)
    f_t = _cumsum_lanes(logf_full_t)
    f_q = jnp.swapaxes(f_t[:, :, pos0:pos0 + t], 1, 2)
    o_fox = _fox_attention(fq, f_q, full['fox'], f_t, pos0)
    o_dsa = _dsa_attention(rope_heads(dq), rope_heads(iq), iw, full['dsa'], pos0,
                           min(DSA_TOPK, seq_len // 4))
    outs = [o.reshape(bsz * t, HEADS * DK) for o in (o_nsa, o_mla, o_fox, o_dsa)]
    return _merge(x2, outs, mg.reshape(bsz * t, -1), lw['w_branch'], lw['w_out']), rows


def _trunk_layer(x2, bsz, t, pos0, past, lw):
    x2, rows = _token_mixers(x2, bsz, t, pos0, past, lw)
    return _mlp(x2, lw['g_mlp'], lw['w_up'], lw['w_down']), rows


def _compress_weights(wk1, wv1):
    hidden = wk1.shape[1]
    k = wk1.reshape(2, CMP_STRIDE, DK, hidden)
    v = wv1.reshape(2, CMP_STRIDE, DK, hidden)
    z = jnp.zeros_like(k[0])
    top = jnp.concatenate([k[0], k[1], z, z], axis=-1)
    bot = jnp.concatenate([z, z, v[0], v[1]], axis=-1)
    return jnp.concatenate([top, bot], axis=1).astype(BF16)


def kernel(x_prompt, x_sample, cache_nsa, cache_mla, cache_fox, cache_fox_logf, cache_dsa, state_nsa_win, page_table, g_mix, w_in, b_fox_f, nsa_pe_k, nsa_pe_v, nsa_wk1, nsa_wk2, nsa_wv1, nsa_wv2, mla_g_ckv, mla_w_uk, mla_w_uv, w_branch, w_out, g_mlp, w_up, w_down, g_final):
    depth = w_in.shape[0]
    bp, tp, d_model = x_prompt.shape
    bs, ts, _ = x_sample.shape
    past_len = page_table.shape[1] * cache_nsa.shape[2]
    d_in = w_in.shape[2]
    d_in_pad = _round_up(d_in, 512)
    logf_pages_t = jnp.swapaxes(cache_fox_logf, 2, 3)
    names = ('nsa', 'mla', 'fox', 'fox_logf', 'dsa', 'win')
    new_p = {n: [] for n in names}
    new_s = {n: [] for n in names}
    xp = x_prompt.reshape(bp * tp, d_model)
    xs = x_sample.reshape(bs * ts, d_model)
    for l in range(depth):
        lw = {
            'g_mix': g_mix[l],
            'w_in': jnp.pad(w_in[l], ((0, 0), (0, d_in_pad - d_in))).astype(BF16),
            'b_fox_f': b_fox_f[l],
            'nsa_pe_k': nsa_pe_k[l].reshape(-1, 1), 'nsa_pe_v': nsa_pe_v[l].reshape(-1, 1),
            'nsa_w1': _compress_weights(nsa_wk1[l], nsa_wv1[l]),
            'nsa_wk1': nsa_wk1[l], 'nsa_wv1': nsa_wv1[l],
            'nsa_wk2': nsa_wk2[l].astype(BF16), 'nsa_wv2': nsa_wv2[l].astype(BF16),
            'mla_g_ckv': mla_g_ckv[l],
            'mla_w_uk': jnp.transpose(mla_w_uk[l], (1, 2, 0)).astype(BF16),
            'mla_w_uv': jnp.transpose(mla_w_uv[l], (1, 0, 2)).astype(BF16),
            'w_branch': w_branch[l].astype(BF16), 'w_out': w_out[l].astype(BF16),
            'g_mlp': g_mlp[l], 'w_up': w_up[l].astype(BF16), 'w_down': w_down[l].astype(BF16),
        }
        past = {'layer': l, 'page_table': page_table, 'nsa': cache_nsa, 'mla': cache_mla, 'fox': cache_fox,
                'fox_logf_t': logf_pages_t, 'dsa': cache_dsa, 'win': state_nsa_win[l]}
        xp, rows_p = _trunk_layer(xp, bp, tp, 0, None, lw)
        xs, rows_s = _trunk_layer(xs, bs, ts, past_len, past, lw)
        for n in names:
            new_p[n].append(rows_p[n])
            new_s[n].append(rows_s[n])
    y_prompt = _rms_norm_rows(xp, g_final).reshape(bp, tp, d_model)
    y_sample = _rms_norm_rows(xs, g_final).reshape(bs, ts, d_model)
    return (y_prompt, y_sample,
            jnp.stack(new_p['nsa']), jnp.stack(new_p['mla']), jnp.stack(new_p['fox']),
            jnp.stack(new_p['fox_logf']), jnp.stack(new_p['dsa']), jnp.stack(new_p['win']),
            jnp.stack(new_s['nsa']), jnp.stack(new_s['mla']), jnp.stack(new_s['fox']),
            jnp.stack(new_s['fox_logf']), jnp.stack(new_s['dsa']), jnp.stack(new_s['win']))
```

```python
import functools
import math

import jax
import jax.numpy as jnp
from jax import lax
from jax.experimental import pallas as pl
from jax.experimental.pallas import tpu as pltpu

F32 = jnp.float32
BF16 = jnp.bfloat16
I32 = jnp.int32

ROPE_THETA = 10000.0
EPS = 1e-6
N_BRANCH = 4
HEADS = 4
DK = 64
CMP_BLOCK = 32
CMP_STRIDE = 16
SLC_BLOCK = 64
SLC_TOPN = 16
WINDOW = 512
MLA_KV_RANK = 256
MLA_NOPE = 64
MLA_ROPE = 32
FOX_KV_HEADS = 2
DSA_TOPK = 256
IDX_HEADS = 4

LANES = 128
NEG = -1e30
BIG = 3e38
SOFTMAX_FLOOR = 1e-30
VMEM_LIMIT = 56 * 1024 * 1024


def _round_up(x, m):
    return -(-x // m) * m


def _pick_tile(n, cap):
    best = LANES
    for t in range(LANES, cap + 1, LANES):
        if n % t == 0:
            best = t
    return best


def _params(*sem):
    return pltpu.CompilerParams(dimension_semantics=sem, vmem_limit_bytes=VMEM_LIMIT)


def _dot(a, b):
    return jnp.dot(a.astype(BF16), b.astype(BF16), preferred_element_type=F32)


def _dot_nt(a, b):
    return lax.dot_general(a.astype(BF16), b.astype(BF16), (((1,), (1,)), ((), ())),
                           preferred_element_type=F32)


def _split3(a):
    hi = a.astype(BF16)
    r1 = a - hi.astype(F32)
    mid = r1.astype(BF16)
    lo = (r1 - mid.astype(F32)).astype(BF16)
    return hi, mid, lo


def _dot_split3(a, b01):
    hi, mid, lo = _split3(a)
    out = jnp.dot(hi, b01, preferred_element_type=F32)
    out = out + jnp.dot(mid, b01, preferred_element_type=F32)
    return out + jnp.dot(lo, b01, preferred_element_type=F32)


def _stack_heads(x, width):
    return jnp.concatenate([x[:, h * width:(h + 1) * width] for h in range(HEADS)], axis=0)


def _tile_rows(x, n):
    return jnp.concatenate([x] * n, axis=0)


def _softmax_step(s, mask, v, carry):
    m, l, acc = carry
    sm = jnp.where(mask, s, NEG)
    m_new = jnp.maximum(m, jnp.max(sm, axis=-1, keepdims=True))
    p = jnp.where(mask, jnp.exp(sm - m_new), 0.0)
    alpha = jnp.exp(m - m_new)
    l = alpha * l + jnp.sum(p, axis=-1, keepdims=True)
    acc = alpha * acc + _dot(p, v)
    return m_new, l, acc


def _softmax_init(rows, dv):
    return (jnp.full((rows, 1), NEG, F32), jnp.zeros((rows, 1), F32), jnp.zeros((rows, dv), F32))


def _softmax_done(carry):
    _, l, acc = carry
    return acc / jnp.maximum(l, SOFTMAX_FLOOR)


def _count(pred):
    return jnp.sum(jnp.where(pred, 1, 0).astype(I32), axis=-1, keepdims=True)


def _select_topk(score, cand, k, n_bisect=24):
    rows, n = score.shape
    sc = jnp.where(cand, score, -BIG)
    ncand = _count(cand)
    small = ncand <= k
    lo0 = jnp.min(jnp.where(cand, score, BIG), axis=-1, keepdims=True)
    top = jnp.max(sc, axis=-1, keepdims=True)
    done0 = jnp.where(small, 1.0, jnp.where(_count(sc >= top) >= k, 1.0, 0.0))

    def bisect(_, c):
        lo, hi = c
        mid = 0.5 * (lo + hi)
        ge = _count(sc >= mid) >= k
        return jnp.where(ge, mid, lo), jnp.where(ge, hi, mid)

    _, hi = lax.fori_loop(0, n_bisect, bisect, (lo0, top))

    def pending(c):
        return jnp.min(c[2]) < 0.5

    def step_down(c):
        hi, t, done = c
        nxt = jnp.max(jnp.where(sc < hi, sc, -BIG), axis=-1, keepdims=True)
        fin = _count(sc >= nxt) >= k
        is_done = done > 0.5
        t = jnp.where(is_done, t, nxt)
        hi = jnp.where(is_done, hi, nxt)
        done = jnp.where(fin, 1.0, done)
        return hi, t, done

    _, t, _ = lax.while_loop(pending, step_down, (hi, top, done0))

    above = sc > t
    need = k - _count(above)
    idx = lax.broadcasted_iota(I32, (rows, n), 1)
    tie_idx = jnp.where(sc == t, idx, jnp.int32(1 << 30))
    idx_bits = max(1, (n - 1).bit_length())
    surplus = jnp.where(small, 0, _count(sc == t) - need)

    def idx_search(i, j):
        c = j | jnp.left_shift(jnp.int32(1), idx_bits - 1 - i)
        return jnp.where(_count(tie_idx < c) < need, c, j)

    j = lax.cond(jnp.max(surplus.astype(F32)) > 0.5,
                 lambda: lax.fori_loop(0, idx_bits, idx_search, jnp.zeros((rows, 1), I32)),
                 lambda: jnp.full((rows, 1), n, I32))
    chosen = jnp.where(above, 1.0, jnp.where(tie_idx <= j, 1.0, 0.0))
    return jnp.where(small, jnp.where(cand, 1.0, 0.0), chosen)


def _norm_matmul_kernel(x_ref, g_ref, w_ref, o_ref, h_ref):
    @pl.when(pl.program_id(1) == 0)
    def _():
        x = x_ref[...]
        y = x * lax.rsqrt(jnp.mean(x * x, axis=-1, keepdims=True) + EPS)
        h_ref[...] = (y * g_ref[...]).astype(BF16)

    o_ref[...] = jnp.dot(h_ref[...], w_ref[...], preferred_element_type=F32)


def _norm_matmul(x, g, w):
    m, d = x.shape
    n = w.shape[1]
    tm = math.gcd(m, 1024)
    tn = _pick_tile(n, 512)
    return pl.pallas_call(
        _norm_matmul_kernel,
        grid=(m // tm, n // tn),
        in_specs=[pl.BlockSpec((tm, d), lambda i, j: (i, 0)),
                  pl.BlockSpec((1, d), lambda i, j: (0, 0)),
                  pl.BlockSpec((d, tn), lambda i, j: (0, j))],
        out_specs=pl.BlockSpec((tm, tn), lambda i, j: (i, j)),
        out_shape=jax.ShapeDtypeStruct((m, n), F32),
        scratch_shapes=[pltpu.VMEM((tm, d), BF16)],
        compiler_params=_params("parallel", "arbitrary"),
        name="norm_matmul",
    )(x, g.reshape(1, d), w)


def _merge_kernel(x_ref, o0_ref, o1_ref, o2_ref, o3_ref, mg_ref, wb_ref, wo_ref, y_ref):
    d = x_ref.shape[1]
    merged = None
    for m, o_ref in enumerate((o0_ref, o1_ref, o2_ref, o3_ref)):
        gate = jax.nn.sigmoid(mg_ref[:, m * d:(m + 1) * d])
        term = gate * _dot(o_ref[...], wb_ref[m])
        merged = term if merged is None else merged + term
    y_ref[...] = x_ref[...] + _dot(merged, wo_ref[...])


def _merge(x, outs, mg, wb, wo):
    m, d = x.shape
    bw = outs[0].shape[1]
    tm = math.gcd(m, 256)
    row = lambda i: (i, 0)
    return pl.pallas_call(
        _merge_kernel,
        grid=(m // tm,),
        in_specs=[pl.BlockSpec((tm, d), row)] + [pl.BlockSpec((tm, bw), row)] * N_BRANCH
        + [pl.BlockSpec((tm, N_BRANCH * d), row),
           pl.BlockSpec((N_BRANCH, bw, d), lambda i: (0, 0, 0)),
           pl.BlockSpec((d, d), lambda i: (0, 0))],
        out_specs=pl.BlockSpec((tm, d), row),
        out_shape=jax.ShapeDtypeStruct((m, d), F32),
        compiler_params=_params("parallel"),
        name="merge",
    )(x, *outs, mg, wb, wo)


def _mlp_kernel(x_ref, g_ref, wu_ref, wd_ref, y_ref, h_ref, acc_ref):
    f = pl.program_id(1)

    @pl.when(f == 0)
    def _():
        x = x_ref[...]
        y = x * lax.rsqrt(jnp.mean(x * x, axis=-1, keepdims=True) + EPS)
        h_ref[...] = (y * g_ref[...]).astype(BF16)
        acc_ref[...] = x

    u = jnp.dot(h_ref[...], wu_ref[...], preferred_element_type=F32)
    acc_ref[...] += _dot(jnp.square(jnp.maximum(u, 0.0)), wd_ref[...])

    @pl.when(f == pl.num_programs(1) - 1)
    def _():
        y_ref[...] = acc_ref[...]


def _mlp(x, g, wu, wd):
    m, d = x.shape
    dff = wu.shape[1]
    tm = math.gcd(m, 1024)
    tf = _pick_tile(dff, 512)
    return pl.pallas_call(
        _mlp_kernel,
        grid=(m // tm, dff // tf),
        in_specs=[pl.BlockSpec((tm, d), lambda i, f: (i, 0)),
                  pl.BlockSpec((1, d), lambda i, f: (0, 0)),
                  pl.BlockSpec((d, tf), lambda i, f: (0, f)),
                  pl.BlockSpec((tf, d), lambda i, f: (f, 0))],
        out_specs=pl.BlockSpec((tm, d), lambda i, f: (i, 0)),
        out_shape=jax.ShapeDtypeStruct((m, d), F32),
        scratch_shapes=[pltpu.VMEM((tm, d), BF16), pltpu.VMEM((tm, d), F32)],
        compiler_params=_params("parallel", "arbitrary"),
        name="mlp",
    )(x, g.reshape(1, d), wu, wd)


def _rms_kernel(x_ref, g_ref, y_ref):
    x = x_ref[...]
    y_ref[...] = x * lax.rsqrt(jnp.mean(x * x, axis=-1, keepdims=True) + EPS) * g_ref[...]


def _rms_norm_rows(x, g):
    m, d = x.shape
    tm = math.gcd(m, 1024)
    return pl.pallas_call(
        _rms_kernel,
        grid=(m // tm,),
        in_specs=[pl.BlockSpec((tm, d), lambda i: (i, 0)), pl.BlockSpec((1, d), lambda i: (0, 0))],
        out_specs=pl.BlockSpec((tm, d), lambda i: (i, 0)),
        out_shape=jax.ShapeDtypeStruct((m, d), F32),
        compiler_params=_params("parallel"),
        name="final_norm",
    )(x, g.reshape(1, d))


def _fetch_pages(pt_ref, streams, *, layer, n_pages, page):
    b = pl.program_id(0)
    slot = lax.rem(b, 2)

    def copies(seq, slot):
        out = []
        for cache_ref, buf_ref, sem_ref, col0, width in streams:
            for p in range(n_pages):
                src = cache_ref.at[layer, pt_ref[seq, p]]
                if col0 is not None:
                    src = src.at[:, pl.ds(col0, width)]
                out.append(pltpu.make_async_copy(src, buf_ref.at[slot, pl.ds(p * page, page)],
                                                 sem_ref.at[slot]))
        return out

    @pl.when(b == 0)
    def _():
        for c in copies(0, 0):
            c.start()

    @pl.when(b + 1 < pl.num_programs(0))
    def _():
        for c in copies(b + 1, 1 - slot):
            c.start()

    for c in copies(b, slot):
        c.wait()
    return slot


def _write_tail(buf_ref, slot, start, page, new):
    buf_ref[slot, pl.ds(start, page), :] = jnp.zeros((page, buf_ref.shape[2]), F32)
    buf_ref[slot, pl.ds(start, new.shape[0]), :] = new


def _paged_call(kern, name, page_table, blocked, resident, hbm, scratch, out_width):
    bsz = page_table.shape[0]
    t = blocked[0].shape[1]
    seq = lambda a: pl.BlockSpec((1,) + a.shape[1:], lambda b, pt: (b,) + (0,) * (a.ndim - 1))
    full = lambda a: pl.BlockSpec(a.shape, lambda b, pt: (0,) * a.ndim)
    return pl.pallas_call(
        kern,
        grid_spec=pltpu.PrefetchScalarGridSpec(
            num_scalar_prefetch=1,
            grid=(bsz,),
            in_specs=[seq(a) for a in blocked] + [full(a) for a in resident]
            + [pl.BlockSpec(memory_space=pl.ANY)] * len(hbm),
            out_specs=pl.BlockSpec((1, t, out_width), lambda b, pt: (b, 0, 0)),
            scratch_shapes=scratch),
        out_shape=jax.ShapeDtypeStruct((bsz, t, out_width), F32),
        compiler_params=_params("arbitrary"),
        name=name,
    )(page_table, *blocked, *resident, *hbm)


def _page_buffers(rows, width):
    return [pltpu.VMEM((2, rows, width), F32), pltpu.SemaphoreType.DMA((2,))]


def _fox_cumsum(logf_get, f_ref):
    r = lax.broadcasted_iota(I32, (LANES, LANES), 0)
    c = lax.broadcasted_iota(I32, (LANES, LANES), 1)
    lower = jnp.where(c <= r, 1.0, 0.0).astype(BF16)

    def body(p, carry):
        sl = pl.ds(pl.multiple_of(p * LANES, LANES), LANES)
        hi, mid, lo = _split3(logf_get(sl))
        w = jnp.dot(lower, hi, preferred_element_type=F32)
        w = w + jnp.dot(lower, mid, preferred_element_type=F32)
        w = w + jnp.dot(lower, lo, preferred_element_type=F32) + carry
        f_ref[sl, :] = w
        return w[LANES - 1:LANES, :]

    lax.fori_loop(0, f_ref.shape[0] // LANES, body, jnp.zeros((1, f_ref.shape[1]), F32))


def _fox_core(q, f_ref, kv_get, o_ref, *, tq, tk, qstart):
    n_tiles = (qstart + tq - 1) // tk + 1
    group = HEADS // FOX_KV_HEADS
    kvw = FOX_KV_HEADS * DK
    scale = DK ** -0.5
    fq = f_ref[pl.ds(qstart if isinstance(qstart, int) else pl.multiple_of(qstart, tq), tq), :]
    qpos = _tile_rows(qstart + lax.broadcasted_iota(I32, (tq, 1), 0), group)
    lane = lax.broadcasted_iota(I32, (tq, HEADS), 1)
    for hk in range(FOX_KV_HEADS):
        heads = range(hk * group, (hk + 1) * group)
        qs = jnp.concatenate([q[:, h * DK:(h + 1) * DK] for h in heads], axis=0)
        fqs = jnp.concatenate([fq[:, h:h + 1] for h in heads], axis=0)
        onehot = jnp.concatenate([jnp.where(lane == h, 1.0, 0.0) for h in heads], axis=0).astype(BF16)

        def body(j, carry, hk=hk, qs=qs, fqs=fqs, onehot=onehot):
            sl = pl.ds(pl.multiple_of(j * tk, tk), tk)
            k = kv_get(sl, hk * DK, (hk + 1) * DK)
            v = kv_get(sl, kvw + hk * DK, kvw + (hk + 1) * DK)
            fh, fm, fl = _split3(f_ref[sl, :])
            fk = _dot_nt(onehot, fh) + _dot_nt(onehot, fm) + _dot_nt(onehot, fl)
            s = _dot_nt(qs, k) * scale + (fqs - fk)
            kpos = j * tk + lax.broadcasted_iota(I32, (1, tk), 1)
            return _softmax_step(s, kpos <= qpos, v, carry)

        o = _softmax_done(lax.fori_loop(0, n_tiles, body, _softmax_init(group * tq, DK)))
        for g, h in enumerate(heads):
            o_ref[0, :, h * DK:(h + 1) * DK] = o[g * tq:(g + 1) * tq]


def _fox_kernel(q_ref, lf_ref, kv_ref, o_ref, f_ref, *, tq, tk):
    @pl.when(pl.program_id(1) == 0)
    def _():
        _fox_cumsum(lambda sl: lf_ref[0, sl, :], f_ref)

    _fox_core(q_ref[0], f_ref, lambda sl, c0, c1: kv_ref[0, sl, c0:c1], o_ref,
              tq=tq, tk=tk, qstart=pl.program_id(1) * tq)


def _fox_attention(q, logf, kv):
    bsz, t, _ = q.shape
    tq = math.gcd(t, LANES)
    tk = _pick_tile(t, 768)
    kern = functools.partial(_fox_kernel, tq=tq, tk=tk)
    return pl.pallas_call(
        kern,
        grid=(bsz, t // tq),
        in_specs=[pl.BlockSpec((1, tq, HEADS * DK), lambda b, i: (b, i, 0)),
                  pl.BlockSpec((1, t, HEADS), lambda b, i: (b, 0, 0)),
                  pl.BlockSpec((1, t, kv.shape[2]), lambda b, i: (b, 0, 0))],
        out_specs=pl.BlockSpec((1, tq, HEADS * DK), lambda b, i: (b, i, 0)),
        out_shape=jax.ShapeDtypeStruct((bsz, t, HEADS * DK), F32),
        scratch_shapes=[pltpu.VMEM((t, HEADS), F32)],
        compiler_params=_params("parallel", "arbitrary"),
        name="fox_attention",
    )(q, logf, kv)


def _fox_paged_kernel(pt_ref, q_ref, new_ref, newlf_ref, cache_ref, lcache_ref, o_ref,
                      kvbuf, kvsem, lbuf, lsem, f_ref, *, layer, n_pages, page, tk):
    slot = _fetch_pages(pt_ref, [(cache_ref, kvbuf, kvsem, None, None), (lcache_ref, lbuf, lsem, None, None)],
                        layer=layer, n_pages=n_pages, page=page)
    past = n_pages * page
    _write_tail(kvbuf, slot, past, page, new_ref[0])
    _write_tail(lbuf, slot, past, page, newlf_ref[0])
    _fox_cumsum(lambda sl: lbuf[slot, sl, :], f_ref)
    _fox_core(q_ref[0], f_ref, lambda sl, c0, c1: kvbuf[slot, sl, c0:c1], o_ref,
              tq=q_ref.shape[1], tk=tk, qstart=past)


def _fox_attention_paged(q, new, newlf, cache, lcache, page_table, layer):
    n_pages, page = page_table.shape[1], cache.shape[2]
    lp = (n_pages + 1) * page
    kern = functools.partial(_fox_paged_kernel, layer=layer, n_pages=n_pages, page=page, tk=_pick_tile(lp, 768))
    scratch = _page_buffers(lp, cache.shape[3]) + _page_buffers(lp, HEADS) + [pltpu.VMEM((lp, HEADS), F32)]
    return _paged_call(kern, "fox_attention_paged", page_table, [q, new, newlf], [], [cache, lcache],
                       scratch, HEADS * DK)


def _mla_core(qn, qr, wuk_ref, wuv_ref, kv_get, o_ref, *, tq, tk, qstart):
    n_tiles = (qstart + tq - 1) // tk + 1
    scale = (MLA_NOPE + MLA_ROPE) ** -0.5
    parts = []
    for h in range(HEADS):
        q_lat = _dot(qn[:, h * MLA_NOPE:(h + 1) * MLA_NOPE], wuk_ref[h])
        parts.append(jnp.concatenate([q_lat, qr[:, h * MLA_ROPE:(h + 1) * MLA_ROPE]], axis=-1))
    qs = jnp.concatenate(parts, axis=0).astype(BF16)
    qpos = _tile_rows(qstart + lax.broadcasted_iota(I32, (tq, 1), 0), HEADS)

    def body(j, carry):
        kt = kv_get(pl.ds(pl.multiple_of(j * tk, tk), tk)).astype(BF16)
        s = _dot_nt(qs, kt) * scale
        kpos = j * tk + lax.broadcasted_iota(I32, (1, tk), 1)
        return _softmax_step(s, kpos <= qpos, kt[:, :MLA_KV_RANK], carry)

    o_lat = _softmax_done(lax.fori_loop(0, n_tiles, body, _softmax_init(HEADS * tq, MLA_KV_RANK)))
    for h in range(HEADS):
        o_ref[0, :, h * DK:(h + 1) * DK] = _dot(o_lat[h * tq:(h + 1) * tq], wuv_ref[h])


def _mla_kernel(qn_ref, qr_ref, kv_ref, wuk_ref, wuv_ref, o_ref, *, tq, tk):
    _mla_core(qn_ref[0], qr_ref[0], wuk_ref, wuv_ref, lambda sl: kv_ref[0, sl, :], o_ref,
              tq=tq, tk=tk, qstart=pl.program_id(1) * tq)


def _mla_attention(qn, qr, kv, wuk, wuv):
    bsz, t, _ = qn.shape
    tq = math.gcd(t, LANES)
    tk = _pick_tile(t, 768)
    kern = functools.partial(_mla_kernel, tq=tq, tk=tk)
    return pl.pallas_call(
        kern,
        grid=(bsz, t // tq),
        in_specs=[pl.BlockSpec((1, tq, qn.shape[2]), lambda b, i: (b, i, 0)),
                  pl.BlockSpec((1, tq, qr.shape[2]), lambda b, i: (b, i, 0)),
                  pl.BlockSpec((1, t, kv.shape[2]), lambda b, i: (b, 0, 0)),
                  pl.BlockSpec(wuk.shape, lambda b, i: (0, 0, 0)),
                  pl.BlockSpec(wuv.shape, lambda b, i: (0, 0, 0))],
        out_specs=pl.BlockSpec((1, tq, HEADS * DK), lambda b, i: (b, i, 0)),
        out_shape=jax.ShapeDtypeStruct((bsz, t, HEADS * DK), F32),
        compiler_params=_params("parallel", "arbitrary"),
        name="mla_attention",
    )(qn, qr, kv, wuk, wuv)


def _mla_paged_kernel(pt_ref, qn_ref, qr_ref, new_ref, wuk_ref, wuv_ref, cache_ref, o_ref, buf, sem,
                      *, layer, n_pages, page, tk):
    slot = _fetch_pages(pt_ref, [(cache_ref, buf, sem, None, None)], layer=layer, n_pages=n_pages, page=page)
    past = n_pages * page
    _write_tail(buf, slot, past, page, new_ref[0])
    _mla_core(qn_ref[0], qr_ref[0], wuk_ref, wuv_ref, lambda sl: buf[slot, sl, :], o_ref,
              tq=qn_ref.shape[1], tk=tk, qstart=past)


def _mla_attention_paged(qn, qr, new, wuk, wuv, cache, page_table, layer):
    n_pages, page = page_table.shape[1], cache.shape[2]
    lp = (n_pages + 1) * page
    kern = functools.partial(_mla_paged_kernel, layer=layer, n_pages=n_pages, page=page, tk=_pick_tile(lp, 768))
    return _paged_call(kern, "mla_attention_paged", page_table, [qn, qr, new], [wuk, wuv], [cache],
                       _page_buffers(lp, cache.shape[3]), HEADS * DK)


def _dsa_core(q, qi, wi, kv_get, o_ref, sel_ref, *, tq, tk, lk, qstart, topk):
    n_tiles = (qstart + tq - 1) // tk + 1
    qpos1 = qstart + lax.broadcasted_iota(I32, (tq, 1), 0)
    qpos = _tile_rows(qpos1, HEADS)

    qis = _stack_heads(qi, DK).astype(BF16)
    wi = wi * IDX_HEADS ** -0.5
    tiles = []
    for j in range(lk // tk):
        kidx = kv_get(pl.ds(j * tk, tk), 2 * DK, 3 * DK)
        rel = jnp.maximum(_dot_nt(qis, kidx) * DK ** -0.5, 0.0)
        sc = wi[:, 0:1] * rel[0:tq]
        for h in range(1, IDX_HEADS):
            sc = sc + wi[:, h:h + 1] * rel[h * tq:(h + 1) * tq]
        tiles.append(sc)
    score = jnp.concatenate(tiles, axis=-1)
    causal = lax.broadcasted_iota(I32, (1, lk), 1) <= qpos1
    sel_ref[...] = _select_topk(score, causal, jnp.full((tq, 1), topk, I32))

    qs = _stack_heads(q, DK).astype(BF16)
    scale = DK ** -0.5

    def body(j, carry):
        sl = pl.ds(pl.multiple_of(j * tk, tk), tk)
        s = _dot_nt(qs, kv_get(sl, 0, DK)) * scale
        kpos = j * tk + lax.broadcasted_iota(I32, (1, tk), 1)
        mask = jnp.where(kpos <= qpos, _tile_rows(sel_ref[:, sl], HEADS), 0.0) > 0.5
        return _softmax_step(s, mask, kv_get(sl, DK, 2 * DK), carry)

    o = _softmax_done(lax.fori_loop(0, n_tiles, body, _softmax_init(HEADS * tq, DK)))
    for h in range(HEADS):
        o_ref[0, :, h * DK:(h + 1) * DK] = o[h * tq:(h + 1) * tq]


def _dsa_kernel(q_ref, qi_ref, wi_ref, kv_ref, o_ref, sel_ref, *, tq, tk, q0, topk):
    _dsa_core(q_ref[0], qi_ref[0], wi_ref[0], lambda sl, c0, c1: kv_ref[0, sl, c0:c1], o_ref, sel_ref,
              tq=tq, tk=tk, lk=kv_ref.shape[1], qstart=q0 + pl.program_id(1) * tq, topk=topk)


def _dsa_attention(q, qi, wi, kv, topk):
    bsz, t, _ = q.shape
    tq = math.gcd(t, LANES)
    tk = _pick_tile(t, 512)
    per_seg = tk // tq
    outs = []
    for seg in range(t // tk):
        lk = (seg + 1) * tk
        qmap = lambda b, i, seg=seg: (b, seg * per_seg + i, 0)
        kern = functools.partial(_dsa_kernel, tq=tq, tk=tk, q0=seg * tk, topk=topk)
        outs.append(pl.pallas_call(
            kern,
            grid=(bsz, per_seg),
            in_specs=[pl.BlockSpec((1, tq, HEADS * DK), qmap),
                      pl.BlockSpec((1, tq, HEADS * DK), qmap),
                      pl.BlockSpec((1, tq, IDX_HEADS), qmap),
                      pl.BlockSpec((1, lk, kv.shape[2]), lambda b, i: (b, 0, 0))],
            out_specs=pl.BlockSpec((1, tq, HEADS * DK), lambda b, i: (b, i, 0)),
            out_shape=jax.ShapeDtypeStruct((bsz, tk, HEADS * DK), F32),
            scratch_shapes=[pltpu.VMEM((tq, lk), F32)],
            compiler_params=_params("parallel", "arbitrary"),
            name="dsa_attention",
        )(q, qi, wi, kv))
    return jnp.concatenate(outs, axis=1)


def _dsa_paged_kernel(pt_ref, q_ref, qi_ref, wi_ref, new_ref, cache_ref, o_ref, buf, sem, sel_ref,
                      *, layer, n_pages, page, tk, topk):
    slot = _fetch_pages(pt_ref, [(cache_ref, buf, sem, None, None)], layer=layer, n_pages=n_pages, page=page)
    past = n_pages * page
    _write_tail(buf, slot, past, page, new_ref[0])
    _dsa_core(q_ref[0], qi_ref[0], wi_ref[0], lambda sl, c0, c1: buf[slot, sl, c0:c1], o_ref, sel_ref,
              tq=q_ref.shape[1], tk=tk, lk=past + page, qstart=past, topk=topk)


def _dsa_attention_paged(q, qi, wi, new, cache, page_table, layer, topk):
    n_pages, page = page_table.shape[1], cache.shape[2]
    lp = (n_pages + 1) * page
    kern = functools.partial(_dsa_paged_kernel, layer=layer, n_pages=n_pages, page=page,
                             tk=_pick_tile(lp, 768), topk=topk)
    scratch = _page_buffers(lp, cache.shape[3]) + [pltpu.VMEM((q.shape[1], lp), F32)]
    return _paged_call(kern, "dsa_attention_paged", page_table, [q, qi, wi, new], [], [cache],
                       scratch, HEADS * DK)


def _compress_core(chunk_rows, n_ch, w1_ref, pek_ref, pev_ref, wk1_ref, wv1_ref, wk2_ref, wv2_ref, out_ref):
    acc = jnp.zeros((n_ch, 4 * DK), F32)
    for r in range(CMP_STRIDE):
        acc = acc + _dot(chunk_rows(r), w1_ref[r])
    nxt = pltpu.roll(acc, n_ch - 1, 0)
    hk = acc[:, 0:DK] + nxt[:, DK:2 * DK] + jnp.sum(pek_ref[...] * wk1_ref[...], axis=0, keepdims=True)
    hv = acc[:, 2 * DK:3 * DK] + nxt[:, 3 * DK:4 * DK] + jnp.sum(pev_ref[...] * wv1_ref[...], axis=0, keepdims=True)
    out_ref[...] = jnp.zeros_like(out_ref)
    out_ref[0:n_ch, 0:DK] = _dot(hk * jax.nn.sigmoid(hk), wk2_ref[...])
    out_ref[0:n_ch, DK:2 * DK] = _dot(hv * jax.nn.sigmoid(hv), wv2_ref[...])


def _compress_kernel(x_ref, w1_ref, pek_ref, pev_ref, wk1_ref, wv1_ref, wk2_ref, wv2_ref, o_ref, *, n_ch):
    _compress_core(lambda r: x_ref[0, pl.ds(r, n_ch, stride=CMP_STRIDE), :], n_ch,
                   w1_ref, pek_ref, pev_ref, wk1_ref, wv1_ref, wk2_ref, wv2_ref, o_ref.at[0])


def _nsa_compress(kv, cw):
    bsz, lp, _ = kv.shape
    n_ch = lp // CMP_STRIDE
    ncp = _round_up(n_ch, LANES)
    full = lambda a: pl.BlockSpec(a.shape, lambda b: (0,) * a.ndim)
    kern = functools.partial(_compress_kernel, n_ch=n_ch)
    return pl.pallas_call(
        kern,
        grid=(bsz,),
        in_specs=[pl.BlockSpec((1, lp, 2 * DK), lambda b: (b, 0, 0))] + [full(a) for a in cw],
        out_specs=pl.BlockSpec((1, ncp, 2 * DK), lambda b: (b, 0, 0)),
        out_shape=jax.ShapeDtypeStruct((bsz, ncp, 2 * DK), F32),
        compiler_params=_params("parallel"),
        name="nsa_compress",
    )(kv, *cw)


def _nsa_core(q, g, cmp_ref, slc_get, win_get, o_ref, *, tq, tk, tw, qstart, wpos0, n_sel):
    ncp = cmp_ref.shape[0]
    nsp = _round_up(-(-ncp * CMP_STRIDE // SLC_BLOCK) + 1, LANES)
    qpos1 = qstart + lax.broadcasted_iota(I32, (tq, 1), 0)
    qpos = _tile_rows(qpos1, HEADS)
    qs = _stack_heads(q, DK).astype(BF16)
    scale = DK ** -0.5
    rows = HEADS * tq

    s_c = _dot_nt(qs, cmp_ref[:, 0:DK]) * scale
    cmp_end = lax.broadcasted_iota(I32, (1, ncp), 1) * CMP_STRIDE + (CMP_BLOCK - 1)
    m_c = cmp_end <= qpos
    sm = jnp.where(m_c, s_c, NEG)
    e = jnp.where(m_c, jnp.exp(sm - jnp.max(sm, axis=-1, keepdims=True)), 0.0)
    p_c = e / jnp.maximum(jnp.sum(e, axis=-1, keepdims=True), SOFTMAX_FLOOR)
    o_c = _dot(p_c, cmp_ref[:, DK:2 * DK])

    imp = p_c[0:tq]
    for h in range(1, HEADS):
        imp = imp + p_c[h * tq:(h + 1) * tq]
    r_s = SLC_BLOCK // CMP_STRIDE
    r_c = CMP_BLOCK // CMP_STRIDE
    ci = lax.broadcasted_iota(I32, (ncp, nsp), 0)
    bj = lax.broadcasted_iota(I32, (ncp, nsp), 1) * r_s
    overlap = jnp.where(ci >= bj - (r_c - 1), jnp.where(ci <= bj + (r_s - 1), 1.0, 0.0), 0.0)
    imp_s = _dot_split3(imp, overlap.astype(BF16))
    blk = lax.broadcasted_iota(I32, (1, nsp), 1)
    cur = jnp.right_shift(qpos1, int(math.log2(SLC_BLOCK)))
    forced = (blk == 0) | (blk == cur) | (blk == cur - 1)
    cand = (blk <= cur) & jnp.logical_not(forced)
    chosen = _select_topk(imp_s, cand, n_sel - _count(forced))
    sel = jnp.where(forced, 1.0, chosen).astype(BF16)

    def slc_body(j, carry):
        k, v = slc_get(pl.ds(pl.multiple_of(j * tk, tk), tk))
        s = _dot_nt(qs, k) * scale
        kpos = j * tk + lax.broadcasted_iota(I32, (1, tk), 1)
        kblk = jnp.right_shift(j * tk + lax.broadcasted_iota(I32, (nsp, tk), 1), int(math.log2(SLC_BLOCK)))
        expand = jnp.where(lax.broadcasted_iota(I32, (nsp, tk), 0) == kblk, 1.0, 0.0).astype(BF16)
        picked = jnp.dot(sel, expand, preferred_element_type=F32)
        mask = _tile_rows(jnp.where(kpos <= qpos1, picked, 0.0), HEADS) > 0.5
        return _softmax_step(s, mask, v, carry)

    n_tiles = (qstart + tq - 1) // tk + 1
    o_s = _softmax_done(lax.fori_loop(0, n_tiles, slc_body, _softmax_init(rows, DK)))

    def win_body(j, carry):
        k, v = win_get(pl.ds(pl.multiple_of(j * tw, tw), tw))
        s = _dot_nt(qs, k) * scale
        kpos = wpos0 + j * tw + lax.broadcasted_iota(I32, (1, tw), 1)
        mask = (kpos <= qpos) & (kpos >= qpos - WINDOW) & (kpos >= 0)
        return _softmax_step(s, mask, v, carry)

    lo_t = jnp.maximum(qstart - WINDOW - wpos0, 0) // tw
    hi_t = (qstart + tq - 1 - wpos0) // tw + 1
    o_w = _softmax_done(lax.fori_loop(lo_t, hi_t, win_body, _softmax_init(rows, DK)))

    for h in range(HEADS):
        hs = slice(h * tq, (h + 1) * tq)
        o_ref[0, :, h * DK:(h + 1) * DK] = (g[:, 3 * h:3 * h + 1] * o_c[hs] + g[:, 3 * h + 1:3 * h + 2] * o_s[hs]
                                            + g[:, 3 * h + 2:3 * h + 3] * o_w[hs])


def _nsa_kernel(q_ref, g_ref, kv_ref, cmp_ref, win_ref, o_ref, *, tq, tk, tw, n_sel):
    _nsa_core(q_ref[0], g_ref[0], cmp_ref.at[0],
              lambda sl: (kv_ref[0, sl, 2 * DK:3 * DK], kv_ref[0, sl, 3 * DK:4 * DK]),
              lambda sl: (win_ref[0, sl, 0:DK], win_ref[0, sl, DK:2 * DK]),
              o_ref, tq=tq, tk=tk, tw=tw, qstart=pl.program_id(1) * tq, wpos0=0, n_sel=n_sel)


def _nsa_attention(q, g, kv, cmp, win, n_sel):
    bsz, t, _ = q.shape
    tq = math.gcd(t, LANES)
    tk = _pick_tile(t, 768)
    kern = functools.partial(_nsa_kernel, tq=tq, tk=tk, tw=LANES, n_sel=n_sel)
    return pl.pallas_call(
        kern,
        grid=(bsz, t // tq),
        in_specs=[pl.BlockSpec((1, tq, HEADS * DK), lambda b, i: (b, i, 0)),
                  pl.BlockSpec((1, tq, g.shape[2]), lambda b, i: (b, i, 0)),
                  pl.BlockSpec((1, t, kv.shape[2]), lambda b, i: (b, 0, 0)),
                  pl.BlockSpec((1,) + cmp.shape[1:], lambda b, i: (b, 0, 0)),
                  pl.BlockSpec((1,) + win.shape[1:], lambda b, i: (b, 0, 0))],
        out_specs=pl.BlockSpec((1, tq, HEADS * DK), lambda b, i: (b, i, 0)),
        out_shape=jax.ShapeDtypeStruct((bsz, t, HEADS * DK), F32),
        compiler_params=_params("parallel", "arbitrary"),
        name="nsa_attention",
    )(q, g, kv, cmp, win)


def _nsa_paged_kernel(pt_ref, q_ref, g_ref, new_ref, win_ref, w1_ref, pek_ref, pev_ref, wk1_ref, wv1_ref,
                      wk2_ref, wv2_ref, cache_ref, o_ref, cbuf, csem, sbuf, ssem, cmp_ref,
                      *, layer, n_pages, page, tk, wpos0, n_sel):
    half = 2 * DK
    slot = _fetch_pages(pt_ref, [(cache_ref, cbuf, csem, 0, half), (cache_ref, sbuf, ssem, half, half)],
                        layer=layer, n_pages=n_pages, page=page)
    past = n_pages * page
    new = new_ref[0]
    _write_tail(cbuf, slot, past, page, new[:, 0:half])
    _write_tail(sbuf, slot, past, page, new[:, half:2 * half])
    n_ch = (past + page) // CMP_STRIDE
    _compress_core(lambda r: cbuf[slot, pl.ds(r, n_ch, stride=CMP_STRIDE), :], n_ch,
                   w1_ref, pek_ref, pev_ref, wk1_ref, wv1_ref, wk2_ref, wv2_ref, cmp_ref)
    _nsa_core(q_ref[0], g_ref[0], cmp_ref,
              lambda sl: (sbuf[slot, sl, 0:DK], sbuf[slot, sl, DK:2 * DK]),
              lambda sl: (win_ref[0, sl, 0:DK], win_ref[0, sl, DK:2 * DK]),
              o_ref, tq=q_ref.shape[1], tk=tk, tw=LANES, qstart=past, wpos0=wpos0, n_sel=n_sel)


def _nsa_attention_paged(q, g, new, win, cw, cache, page_table, layer, wpos0, n_sel):
    n_pages, page = page_table.shape[1], cache.shape[2]
    lp = (n_pages + 1) * page
    ncp = _round_up(lp // CMP_STRIDE, LANES)
    kern = functools.partial(_nsa_paged_kernel, layer=layer, n_pages=n_pages, page=page,
                             tk=_pick_tile(lp, 768), wpos0=wpos0, n_sel=n_sel)
    scratch = _page_buffers(lp, 2 * DK) + _page_buffers(lp, 2 * DK) + [pltpu.VMEM((ncp, 2 * DK), F32)]
    return _paged_call(kern, "nsa_attention_paged", page_table, [q, g, new, win], list(cw), [cache],
                       scratch, HEADS * DK)


def _rope(x, pos):
    half = x.shape[-1] // 2
    inv = ROPE_THETA ** (-jnp.arange(half, dtype=F32) / half)
    ang = pos.astype(F32)[:, None] * inv[None, :]
    if x.ndim == 4:
        ang = ang[:, None, :]
    cos, sin = jnp.cos(ang), jnp.sin(ang)
    x1, x2 = x[..., :half], x[..., half:]
    return jnp.concatenate([x1 * cos - x2 * sin, x2 * cos + x1 * sin], axis=-1)


def _rms(x, g):
    return x * lax.rsqrt(jnp.mean(x * x, axis=-1, keepdims=True) + EPS) * g


def _in_splits(d_model):
    return (HEADS * DK, DK, DK, DK, DK, DK, DK, 3 * HEADS,
            HEADS * MLA_NOPE, HEADS * MLA_ROPE, MLA_KV_RANK, MLA_ROPE,
            HEADS * DK, FOX_KV_HEADS * DK, FOX_KV_HEADS * DK, HEADS,
            HEADS * DK, DK, DK, IDX_HEADS * DK, DK, IDX_HEADS,
            N_BRANCH * d_model)


def _token_mixers(x2, bsz, t, pos0, past, lw):
    d_model = x2.shape[1]
    splits = _in_splits(d_model)
    proj = _norm_matmul(x2, lw['g_mix'], lw['w_in'])
    cuts = [0]
    for s in splits:
        cuts.append(cuts[-1] + s)
    (nq, nkc, nvc, nks, nvs, nkw, nvw, ng, mqn, mqr, mc, mkr, fq, fk, fv, ff,
     dq, dk_, dv, iq, ik, iw, mg) = [proj[:, a:b].reshape(bsz, t, b - a) for a, b in zip(cuts[:-1], cuts[1:])]
    pos = pos0 + jnp.arange(t, dtype=I32)
    heads4 = lambda a: a.reshape(bsz, t, HEADS, -1)
    rope_heads = lambda a: _rope(heads4(a), pos).reshape(bsz, t, -1)
    rows = {
        'nsa': jnp.concatenate([_rope(nkc, pos), nvc, _rope(nks, pos), nvs], axis=-1),
        'mla': jnp.concatenate([_rms(mc, lw['mla_g_ckv']), _rope(mkr, pos)], axis=-1),
        'fox': jnp.concatenate([fk, fv], axis=-1),
        'fox_logf': jax.nn.log_sigmoid(ff + lw['b_fox_f']),
        'dsa': jnp.concatenate([_rope(dk_, pos), dv, _rope(ik, pos)], axis=-1),
    }
    win_rows = jnp.concatenate([_rope(nkw, pos), nvw], axis=-1)
    win_all = win_rows if past is None else jnp.concatenate([past['win'], win_rows], axis=1)
    seq_len = pos0 + t
    rows['win'] = win_all[:, -min(WINDOW, seq_len):]
    win_pad = _round_up(win_all.shape[1], LANES) - win_all.shape[1]
    win_buf = jnp.pad(win_all, ((0, 0), (0, win_pad), (0, 0)))

    n_sel = min(SLC_TOPN, -(-seq_len // SLC_BLOCK))
    topk = min(DSA_TOPK, seq_len // 4)
    q_nsa, g_nsa = rope_heads(nq), jax.nn.sigmoid(ng)
    q_mla_r = rope_heads(mqr)
    q_dsa, q_idx = rope_heads(dq), rope_heads(iq)
    if past is None:
        cmp = _nsa_compress(rows['nsa'], lw['nsa_cw'])
        o_nsa = _nsa_attention(q_nsa, g_nsa, rows['nsa'], cmp, win_buf, n_sel)
        o_mla = _mla_attention(mqn, q_mla_r, rows['mla'], lw['mla_w_uk'], lw['mla_w_uv'])
        o_fox = _fox_attention(fq, rows['fox_logf'], rows['fox'])
        o_dsa = _dsa_attention(q_dsa, q_idx, iw, rows['dsa'], topk)
    else:
        pt, layer = past['page_table'], past['layer']
        o_nsa = _nsa_attention_paged(q_nsa, g_nsa, rows['nsa'], win_buf, lw['nsa_cw'], past['nsa'], pt, layer,
                                     pos0 - past['win'].shape[1], n_sel)
        o_mla = _mla_attention_paged(mqn, q_mla_r, rows['mla'], lw['mla_w_uk'], lw['mla_w_uv'], past['mla'], pt, layer)
        o_fox = _fox_attention_paged(fq, rows['fox'], rows['fox_logf'], past['fox'], past['fox_logf'], pt, layer)
        o_dsa = _dsa_attention_paged(q_dsa, q_idx, iw, rows['dsa'], past['dsa'], pt, layer, topk)
    outs = [o.reshape(bsz * t, HEADS * DK) for o in (o_nsa, o_mla, o_fox, o_dsa)]
    return _merge(x2, outs, mg.reshape(bsz * t, -1), lw['w_branch'], lw['w_out']), rows


def _trunk_layer(x2, bsz, t, pos0, past, lw):
    x2, rows = _token_mixers(x2, bsz, t, pos0, past, lw)
    return _mlp(x2, lw['g_mlp'], lw['w_up'], lw['w_down']), rows


def _compress_weights(wk1, wv1):
    hidden = wk1.shape[1]
    k = wk1.reshape(2, CMP_STRIDE, DK, hidden)
    v = wv1.reshape(2, CMP_STRIDE, DK, hidden)
    z = jnp.zeros_like(k[0])
    top = jnp.concatenate([k[0], k[1], z, z], axis=-1)
    bot = jnp.concatenate([z, z, v[0], v[1]], axis=-1)
    return jnp.concatenate([top, bot], axis=1).astype(BF16)


def kernel(x_prompt, x_sample, cache_nsa, cache_mla, cache_fox, cache_fox_logf, cache_dsa, state_nsa_win, page_table, g_mix, w_in, b_fox_f, nsa_pe_k, nsa_pe_v, nsa_wk1, nsa_wk2, nsa_wv1, nsa_wv2, mla_g_ckv, mla_w_uk, mla_w_uv, w_branch, w_out, g_mlp, w_up, w_down, g_final):
    depth = w_in.shape[0]
    bp, tp, d_model = x_prompt.shape
    bs, ts, _ = x_sample.shape
    past_len = page_table.shape[1] * cache_nsa.shape[2]
    d_in = w_in.shape[2]
    d_in_pad = _round_up(d_in, 512)
    names = ('nsa', 'mla', 'fox', 'fox_logf', 'dsa', 'win')
    new_p = {n: [] for n in names}
    new_s = {n: [] for n in names}
    xp = x_prompt.reshape(bp * tp, d_model)
    xs = x_sample.reshape(bs * ts, d_model)
    for l in range(depth):
        lw = {
            'g_mix': g_mix[l],
            'w_in': jnp.pad(w_in[l], ((0, 0), (0, d_in_pad - d_in))).astype(BF16),
            'b_fox_f': b_fox_f[l],
            'nsa_cw': (_compress_weights(nsa_wk1[l], nsa_wv1[l]),
                       nsa_pe_k[l].reshape(-1, 1), nsa_pe_v[l].reshape(-1, 1), nsa_wk1[l], nsa_wv1[l],
                       nsa_wk2[l].astype(BF16), nsa_wv2[l].astype(BF16)),
            'mla_g_ckv': mla_g_ckv[l],
            'mla_w_uk': jnp.transpose(mla_w_uk[l], (1, 2, 0)).astype(BF16),
            'mla_w_uv': jnp.transpose(mla_w_uv[l], (1, 0, 2)).astype(BF16),
            'w_branch': w_branch[l].astype(BF16), 'w_out': w_out[l].astype(BF16),
            'g_mlp': g_mlp[l], 'w_up': w_up[l].astype(BF16), 'w_down': w_down[l].astype(BF16),
        }
        past = {'layer': l, 'page_table': page_table, 'nsa': cache_nsa, 'mla': cache_mla, 'fox': cache_fox,
                'fox_logf': cache_fox_logf, 'dsa': cache_dsa, 'win': state_nsa_win[l]}
        xp, rows_p = _trunk_layer(xp, bp, tp, 0, None, lw)
        xs, rows_s = _trunk_layer(xs, bs, ts, past_len, past, lw)
        for n in names:
            new_p[n].append(rows_p[n])
            new_s[n].append(rows_s[n])
    y_prompt = _rms_norm_rows(xp, g_final).reshape(bp, tp, d_model)
    y_sample = _rms_norm_rows(xs, g_final).reshape(bs, ts, d_model)
    return (y_prompt, y_sample,
            jnp.stack(new_p['nsa']), jnp.stack(new_p['mla']), jnp.stack(new_p['fox']),
            jnp.stack(new_p['fox_logf']), jnp.stack(new_p['dsa']), jnp.stack(new_p['win']),
            jnp.stack(new_s['nsa']), jnp.stack(new_s['mla']), jnp.stack(new_s['fox']),
            jnp.stack(new_s['fox_logf']), jnp.stack(new_s['dsa']), jnp.stack(new_s['win']))
```

```python
import functools
import math

import jax
import jax.numpy as jnp
from jax import lax
from jax.experimental import pallas as pl
from jax.experimental.pallas import tpu as pltpu

F32 = jnp.float32
BF16 = jnp.bfloat16
I32 = jnp.int32

ROPE_THETA = 10000.0
EPS = 1e-6
N_BRANCH = 4
HEADS = 4
DK = 64
CMP_BLOCK = 32
CMP_STRIDE = 16
SLC_BLOCK = 64
SLC_TOPN = 16
WINDOW = 512
MLA_KV_RANK = 256
MLA_NOPE = 64
MLA_ROPE = 32
FOX_KV_HEADS = 2
DSA_TOPK = 256
IDX_HEADS = 4

LANES = 128
NEG = -1e30
BIG = 3e38
SOFTMAX_FLOOR = 1e-30
VMEM_LIMIT = 56 * 1024 * 1024


def _round_up(x, m):
    return -(-x // m) * m


def _pick_tile(n, cap):
    best = LANES
    for t in range(LANES, cap + 1, LANES):
        if n % t == 0:
            best = t
    return best


def _params(*sem):
    return pltpu.CompilerParams(dimension_semantics=sem, vmem_limit_bytes=VMEM_LIMIT)


def _dot(a, b):
    return jnp.dot(a.astype(BF16), b.astype(BF16), preferred_element_type=F32)


def _dot_nt(a, b):
    return lax.dot_general(a.astype(BF16), b.astype(BF16), (((1,), (1,)), ((), ())),
                           preferred_element_type=F32)


def _split3(a):
    hi = a.astype(BF16)
    r1 = a - hi.astype(F32)
    mid = r1.astype(BF16)
    lo = (r1 - mid.astype(F32)).astype(BF16)
    return hi, mid, lo


def _dot_split3(a, b01):
    hi, mid, lo = _split3(a)
    out = jnp.dot(hi, b01, preferred_element_type=F32)
    out = out + jnp.dot(mid, b01, preferred_element_type=F32)
    return out + jnp.dot(lo, b01, preferred_element_type=F32)


def _stack_heads(x, width):
    return jnp.concatenate([x[:, h * width:(h + 1) * width] for h in range(HEADS)], axis=0)


def _tile_rows(x, n):
    return jnp.concatenate([x] * n, axis=0)


def _softmax_step(s, mask, v, carry, v_t=False):
    m, l, acc = carry
    sm = jnp.where(mask, s, NEG)
    m_new = jnp.maximum(m, jnp.max(sm, axis=-1, keepdims=True))
    p = jnp.where(mask, jnp.exp(sm - m_new), 0.0)
    alpha = jnp.exp(m - m_new)
    l = alpha * l + jnp.sum(p, axis=-1, keepdims=True)
    acc = alpha * acc + (_dot_nt(p, v) if v_t else _dot(p, v))
    return m_new, l, acc


def _qk(q, k, k_t):
    return _dot(q, k) if k_t else _dot_nt(q, k)


def _softmax_init(rows, dv):
    return (jnp.full((rows, 1), NEG, F32), jnp.zeros((rows, 1), F32), jnp.zeros((rows, dv), F32))


def _softmax_done(carry):
    _, l, acc = carry
    return acc / jnp.maximum(l, SOFTMAX_FLOOR)


def _count(pred):
    return jnp.sum(jnp.where(pred, 1, 0).astype(I32), axis=-1, keepdims=True)


def _select_topk(score, cand, k, n_bisect=24):
    rows, n = score.shape
    sc = jnp.where(cand, score, -BIG)
    ncand = _count(cand)
    small = ncand <= k
    lo0 = jnp.min(jnp.where(cand, score, BIG), axis=-1, keepdims=True)
    top = jnp.max(sc, axis=-1, keepdims=True)
    done0 = jnp.where(small, 1.0, jnp.where(_count(sc >= top) >= k, 1.0, 0.0))

    def bisect(_, c):
        lo, hi = c
        mid = 0.5 * (lo + hi)
        ge = _count(sc >= mid) >= k
        return jnp.where(ge, mid, lo), jnp.where(ge, hi, mid)

    _, hi = lax.fori_loop(0, n_bisect, bisect, (lo0, top))

    def pending(c):
        return jnp.min(c[2]) < 0.5

    def step_down(c):
        hi, t, done = c
        nxt = jnp.max(jnp.where(sc < hi, sc, -BIG), axis=-1, keepdims=True)
        fin = _count(sc >= nxt) >= k
        is_done = done > 0.5
        t = jnp.where(is_done, t, nxt)
        hi = jnp.where(is_done, hi, nxt)
        done = jnp.where(fin, 1.0, done)
        return hi, t, done

    _, t, _ = lax.while_loop(pending, step_down, (hi, top, done0))

    above = sc > t
    need = k - _count(above)
    idx = lax.broadcasted_iota(I32, (rows, n), 1)
    tie_idx = jnp.where(sc == t, idx, jnp.int32(1 << 30))
    idx_bits = max(1, (n - 1).bit_length())
    surplus = jnp.where(small, 0, _count(sc == t) - need)

    def idx_search(i, j):
        c = j | jnp.left_shift(jnp.int32(1), idx_bits - 1 - i)
        return jnp.where(_count(tie_idx < c) < need, c, j)

    j = lax.cond(jnp.max(surplus.astype(F32)) > 0.5,
                 lambda: lax.fori_loop(0, idx_bits, idx_search, jnp.zeros((rows, 1), I32)),
                 lambda: jnp.full((rows, 1), n, I32))
    chosen = jnp.where(above, 1.0, jnp.where(tie_idx <= j, 1.0, 0.0))
    return jnp.where(small, jnp.where(cand, 1.0, 0.0), chosen)


def _norm_matmul_kernel(x_ref, g_ref, w_ref, o_ref, h_ref):
    @pl.when(pl.program_id(1) == 0)
    def _():
        x = x_ref[...]
        y = x * lax.rsqrt(jnp.mean(x * x, axis=-1, keepdims=True) + EPS)
        h_ref[...] = (y * g_ref[...]).astype(BF16)

    o_ref[...] = jnp.dot(h_ref[...], w_ref[...], preferred_element_type=F32)


def _norm_matmul(x, g, w):
    m, d = x.shape
    n = w.shape[1]
    tm = math.gcd(m, 1024)
    tn = _pick_tile(n, 512)
    return pl.pallas_call(
        _norm_matmul_kernel,
        grid=(m // tm, n // tn),
        in_specs=[pl.BlockSpec((tm, d), lambda i, j: (i, 0)),
                  pl.BlockSpec((1, d), lambda i, j: (0, 0)),
                  pl.BlockSpec((d, tn), lambda i, j: (0, j))],
        out_specs=pl.BlockSpec((tm, tn), lambda i, j: (i, j)),
        out_shape=jax.ShapeDtypeStruct((m, n), F32),
        scratch_shapes=[pltpu.VMEM((tm, d), BF16)],
        compiler_params=_params("parallel", "arbitrary"),
        name="norm_matmul",
    )(x, g.reshape(1, d), w)


def _merge_kernel(x_ref, o0_ref, o1_ref, o2_ref, o3_ref, mg_ref, wb_ref, wo_ref, y_ref):
    d = x_ref.shape[1]
    merged = None
    for m, o_ref in enumerate((o0_ref, o1_ref, o2_ref, o3_ref)):
        gate = jax.nn.sigmoid(mg_ref[:, m * d:(m + 1) * d])
        term = gate * _dot(o_ref[...], wb_ref[m])
        merged = term if merged is None else merged + term
    y_ref[...] = x_ref[...] + _dot(merged, wo_ref[...])


def _merge(x, outs, mg, wb, wo):
    m, d = x.shape
    bw = outs[0].shape[1]
    tm = math.gcd(m, 256)
    row = lambda i: (i, 0)
    return pl.pallas_call(
        _merge_kernel,
        grid=(m // tm,),
        in_specs=[pl.BlockSpec((tm, d), row)] + [pl.BlockSpec((tm, bw), row)] * N_BRANCH
        + [pl.BlockSpec((tm, N_BRANCH * d), row),
           pl.BlockSpec((N_BRANCH, bw, d), lambda i: (0, 0, 0)),
           pl.BlockSpec((d, d), lambda i: (0, 0))],
        out_specs=pl.BlockSpec((tm, d), row),
        out_shape=jax.ShapeDtypeStruct((m, d), F32),
        compiler_params=_params("parallel"),
        name="merge",
    )(x, *outs, mg, wb, wo)


def _mlp_kernel(x_ref, g_ref, wu_ref, wd_ref, y_ref, h_ref, acc_ref):
    f = pl.program_id(1)

    @pl.when(f == 0)
    def _():
        x = x_ref[...]
        y = x * lax.rsqrt(jnp.mean(x * x, axis=-1, keepdims=True) + EPS)
        h_ref[...] = (y * g_ref[...]).astype(BF16)
        acc_ref[...] = x

    u = jnp.dot(h_ref[...], wu_ref[...], preferred_element_type=F32)
    acc_ref[...] += _dot(jnp.square(jnp.maximum(u, 0.0)), wd_ref[...])

    @pl.when(f == pl.num_programs(1) - 1)
    def _():
        y_ref[...] = acc_ref[...]


def _mlp(x, g, wu, wd):
    m, d = x.shape
    dff = wu.shape[1]
    tm = math.gcd(m, 1024)
    tf = _pick_tile(dff, 512)
    return pl.pallas_call(
        _mlp_kernel,
        grid=(m // tm, dff // tf),
        in_specs=[pl.BlockSpec((tm, d), lambda i, f: (i, 0)),
                  pl.BlockSpec((1, d), lambda i, f: (0, 0)),
                  pl.BlockSpec((d, tf), lambda i, f: (0, f)),
                  pl.BlockSpec((tf, d), lambda i, f: (f, 0))],
        out_specs=pl.BlockSpec((tm, d), lambda i, f: (i, 0)),
        out_shape=jax.ShapeDtypeStruct((m, d), F32),
        scratch_shapes=[pltpu.VMEM((tm, d), BF16), pltpu.VMEM((tm, d), F32)],
        compiler_params=_params("parallel", "arbitrary"),
        name="mlp",
    )(x, g.reshape(1, d), wu, wd)


def _rms_kernel(x_ref, g_ref, y_ref):
    x = x_ref[...]
    y_ref[...] = x * lax.rsqrt(jnp.mean(x * x, axis=-1, keepdims=True) + EPS) * g_ref[...]


def _rms_norm_rows(x, g):
    m, d = x.shape
    tm = math.gcd(m, 1024)
    return pl.pallas_call(
        _rms_kernel,
        grid=(m // tm,),
        in_specs=[pl.BlockSpec((tm, d), lambda i: (i, 0)), pl.BlockSpec((1, d), lambda i: (0, 0))],
        out_specs=pl.BlockSpec((tm, d), lambda i: (i, 0)),
        out_shape=jax.ShapeDtypeStruct((m, d), F32),
        compiler_params=_params("parallel"),
        name="final_norm",
    )(x, g.reshape(1, d))


def _fetch_pages(pt_ref, streams, *, layer, n_pages, page):
    b = pl.program_id(0)
    slot = lax.rem(b, 2)

    def copies(seq, slot):
        out = []
        for cache_ref, buf_ref, sem_ref, col0, width in streams:
            for p in range(n_pages):
                src = cache_ref.at[layer, pt_ref[seq, p]]
                if col0 == "T":
                    dst = buf_ref.at[slot, :, pl.ds(p * page, page)]
                else:
                    dst = buf_ref.at[slot, pl.ds(p * page, page)]
                    if col0 is not None:
                        src = src.at[:, pl.ds(col0, width)]
                out.append(pltpu.make_async_copy(src, dst, sem_ref.at[slot]))
        return out

    @pl.when(b == 0)
    def _():
        for c in copies(0, 0):
            c.start()

    @pl.when(b + 1 < pl.num_programs(0))
    def _():
        for c in copies(b + 1, 1 - slot):
            c.start()

    for c in copies(b, slot):
        c.wait()
    return slot


def _write_tail(buf_ref, slot, start, page, new):
    buf_ref[slot, pl.ds(start, page), :] = jnp.zeros((page, buf_ref.shape[2]), F32)
    buf_ref[slot, pl.ds(start, new.shape[0]), :] = new


def _write_tail_t(buf_ref, slot, start, page, new_t):
    buf_ref[slot, :, pl.ds(start, page)] = jnp.zeros((buf_ref.shape[1], page), F32)
    buf_ref[slot, :, pl.ds(start, new_t.shape[1])] = new_t


def _paged_call(kern, name, page_table, blocked, resident, hbm, scratch, out_width):
    bsz = page_table.shape[0]
    t = blocked[0].shape[1]
    seq = lambda a: pl.BlockSpec((1,) + a.shape[1:], lambda b, pt: (b,) + (0,) * (a.ndim - 1))
    full = lambda a: pl.BlockSpec(a.shape, lambda b, pt: (0,) * a.ndim)
    return pl.pallas_call(
        kern,
        grid_spec=pltpu.PrefetchScalarGridSpec(
            num_scalar_prefetch=1,
            grid=(bsz,),
            in_specs=[seq(a) for a in blocked] + [full(a) for a in resident]
            + [pl.BlockSpec(memory_space=pl.ANY)] * len(hbm),
            out_specs=pl.BlockSpec((1, t, out_width), lambda b, pt: (b, 0, 0)),
            scratch_shapes=scratch),
        out_shape=jax.ShapeDtypeStruct((bsz, t, out_width), F32),
        compiler_params=_params("arbitrary"),
        name=name,
    )(page_table, *blocked, *resident, *hbm)


def _page_buffers(rows, width):
    return [pltpu.VMEM((2, rows, width), F32), pltpu.SemaphoreType.DMA((2,))]


def _page_buffers_t(rows, width):
    return [pltpu.VMEM((2, width, rows), F32), pltpu.SemaphoreType.DMA((2,))]


def _fox_cumsum(logf_get, f_ref):
    r = lax.broadcasted_iota(I32, (LANES, LANES), 0)
    c = lax.broadcasted_iota(I32, (LANES, LANES), 1)
    lower = jnp.where(c <= r, 1.0, 0.0).astype(BF16)

    def body(p, carry):
        sl = pl.ds(pl.multiple_of(p * LANES, LANES), LANES)
        hi, mid, lo = _split3(logf_get(sl))
        w = jnp.dot(lower, hi, preferred_element_type=F32)
        w = w + jnp.dot(lower, mid, preferred_element_type=F32)
        w = w + jnp.dot(lower, lo, preferred_element_type=F32) + carry
        f_ref[sl, :] = w
        return w[LANES - 1:LANES, :]

    lax.fori_loop(0, f_ref.shape[0] // LANES, body, jnp.zeros((1, f_ref.shape[1]), F32))


def _fox_cumsum_lanes(logf_get, f_ref):
    r = lax.broadcasted_iota(I32, (LANES, LANES), 0)
    c = lax.broadcasted_iota(I32, (LANES, LANES), 1)
    upper = jnp.where(r <= c, 1.0, 0.0).astype(BF16)

    def body(p, carry):
        sl = pl.ds(pl.multiple_of(p * LANES, LANES), LANES)
        w = _dot_split3(logf_get(sl), upper) + carry
        f_ref[:, sl] = w
        return w[:, LANES - 1:LANES]

    lax.fori_loop(0, f_ref.shape[1] // LANES, body, jnp.zeros((f_ref.shape[0], 1), F32))


def _fox_core(q, fq, fk_get, kv_get, o_ref, *, tq, tk, qstart):
    n_tiles = (qstart + tq - 1) // tk + 1
    group = HEADS // FOX_KV_HEADS
    kvw = FOX_KV_HEADS * DK
    scale = DK ** -0.5
    qpos = _tile_rows(qstart + lax.broadcasted_iota(I32, (tq, 1), 0), group)
    for hk in range(FOX_KV_HEADS):
        heads = range(hk * group, (hk + 1) * group)
        qs = jnp.concatenate([q[:, h * DK:(h + 1) * DK] for h in heads], axis=0)
        fqs = jnp.concatenate([fq[:, h:h + 1] for h in heads], axis=0)

        def body(j, carry, hk=hk, heads=heads, qs=qs, fqs=fqs):
            sl = pl.ds(pl.multiple_of(j * tk, tk), tk)
            k = kv_get(sl, hk * DK, (hk + 1) * DK)
            v = kv_get(sl, kvw + hk * DK, kvw + (hk + 1) * DK)
            s = _dot_nt(qs, k) * scale + (fqs - fk_get(sl, heads))
            kpos = j * tk + lax.broadcasted_iota(I32, (1, tk), 1)
            return _softmax_step(s, kpos <= qpos, v, carry)

        o = _softmax_done(lax.fori_loop(0, n_tiles, body, _softmax_init(group * tq, DK)))
        for g, h in enumerate(heads):
            o_ref[0, :, h * DK:(h + 1) * DK] = o[g * tq:(g + 1) * tq]


def _fox_kernel(q_ref, lf_ref, kv_ref, o_ref, f_ref, *, tq, tk):
    @pl.when(pl.program_id(1) == 0)
    def _():
        _fox_cumsum(lambda sl: lf_ref[0, sl, :], f_ref)

    qstart = pl.program_id(1) * tq
    lane = lax.broadcasted_iota(I32, (tq, HEADS), 1)

    def fk_get(sl, heads):
        onehot = jnp.concatenate([jnp.where(lane == h, 1.0, 0.0) for h in heads], axis=0).astype(BF16)
        fh, fm, fl = _split3(f_ref[sl, :])
        return _dot_nt(onehot, fh) + _dot_nt(onehot, fm) + _dot_nt(onehot, fl)

    _fox_core(q_ref[0], f_ref[pl.ds(pl.multiple_of(qstart, tq), tq), :], fk_get,
              lambda sl, c0, c1: kv_ref[0, sl, c0:c1], o_ref, tq=tq, tk=tk, qstart=qstart)


def _fox_attention(q, logf, kv):
    bsz, t, _ = q.shape
    tq = math.gcd(t, LANES)
    tk = _pick_tile(t, 768)
    kern = functools.partial(_fox_kernel, tq=tq, tk=tk)
    return pl.pallas_call(
        kern,
        grid=(bsz, t // tq),
        in_specs=[pl.BlockSpec((1, tq, HEADS * DK), lambda b, i: (b, i, 0)),
                  pl.BlockSpec((1, t, HEADS), lambda b, i: (b, 0, 0)),
                  pl.BlockSpec((1, t, kv.shape[2]), lambda b, i: (b, 0, 0))],
        out_specs=pl.BlockSpec((1, tq, HEADS * DK), lambda b, i: (b, i, 0)),
        out_shape=jax.ShapeDtypeStruct((bsz, t, HEADS * DK), F32),
        scratch_shapes=[pltpu.VMEM((t, HEADS), F32)],
        compiler_params=_params("parallel", "arbitrary"),
        name="fox_attention",
    )(q, logf, kv)


def _fox_paged_kernel(pt_ref, q_ref, new_ref, newlf_ref, cache_ref, lcache_ref, o_ref,
                      kvbuf, kvsem, lbuf, lsem, f_ref, *, layer, n_pages, page, tk):
    slot = _fetch_pages(pt_ref, [(cache_ref, kvbuf, kvsem, None, None), (lcache_ref, lbuf, lsem, "T", None)],
                        layer=layer, n_pages=n_pages, page=page)
    past = n_pages * page
    tq = q_ref.shape[1]
    _write_tail(kvbuf, slot, past, page, new_ref[0])
    _write_tail_t(lbuf, slot, past, page, newlf_ref[0])
    _fox_cumsum_lanes(lambda sl: lbuf[slot, :, sl], f_ref)
    pick = jnp.where(lax.broadcasted_iota(I32, (tq, page), 0) == lax.broadcasted_iota(I32, (tq, page), 1),
                     1.0, 0.0).astype(BF16)
    fh, fm, fl = _split3(f_ref[:, pl.ds(past, page)])
    fq = _dot_nt(pick, fh) + _dot_nt(pick, fm) + _dot_nt(pick, fl)

    def fk_get(sl, heads):
        ft = f_ref[:, sl]
        return jnp.concatenate([jnp.broadcast_to(ft[h:h + 1, :], (tq, ft.shape[1])) for h in heads], axis=0)

    _fox_core(q_ref[0], fq, fk_get, lambda sl, c0, c1: kvbuf[slot, sl, c0:c1], o_ref,
              tq=tq, tk=tk, qstart=past)


def _fox_attention_paged(q, new, newlf_t, cache, lcache_t, page_table, layer):
    n_pages, page = page_table.shape[1], cache.shape[2]
    lp = (n_pages + 1) * page
    kern = functools.partial(_fox_paged_kernel, layer=layer, n_pages=n_pages, page=page, tk=_pick_tile(lp, 768))
    scratch = _page_buffers(lp, cache.shape[3]) + _page_buffers_t(lp, HEADS) + [pltpu.VMEM((HEADS, lp), F32)]
    return _paged_call(kern, "fox_attention_paged", page_table, [q, new, newlf_t], [], [cache, lcache_t],
                       scratch, HEADS * DK)


def _mla_core(qn, qr, wuk_ref, wuv_ref, kv_get, o_ref, *, tq, tk, qstart, k_t=False):
    n_tiles = (qstart + tq - 1) // tk + 1
    scale = (MLA_NOPE + MLA_ROPE) ** -0.5
    parts = []
    for h in range(HEADS):
        q_lat = _dot(qn[:, h * MLA_NOPE:(h + 1) * MLA_NOPE], wuk_ref[h])
        parts.append(jnp.concatenate([q_lat, qr[:, h * MLA_ROPE:(h + 1) * MLA_ROPE]], axis=-1))
    qs = jnp.concatenate(parts, axis=0).astype(BF16)
    qpos = _tile_rows(qstart + lax.broadcasted_iota(I32, (tq, 1), 0), HEADS)

    def body(j, carry):
        kt = kv_get(pl.ds(pl.multiple_of(j * tk, tk), tk)).astype(BF16)
        s = _qk(qs, kt, k_t) * scale
        kpos = j * tk + lax.broadcasted_iota(I32, (1, tk), 1)
        v = kt[:MLA_KV_RANK, :] if k_t else kt[:, :MLA_KV_RANK]
        return _softmax_step(s, kpos <= qpos, v, carry, v_t=k_t)

    o_lat = _softmax_done(lax.fori_loop(0, n_tiles, body, _softmax_init(HEADS * tq, MLA_KV_RANK)))
    for h in range(HEADS):
        o_ref[0, :, h * DK:(h + 1) * DK] = _dot(o_lat[h * tq:(h + 1) * tq], wuv_ref[h])


def _mla_kernel(qn_ref, qr_ref, kv_ref, wuk_ref, wuv_ref, o_ref, *, tq, tk):
    _mla_core(qn_ref[0], qr_ref[0], wuk_ref, wuv_ref, lambda sl: kv_ref[0, sl, :], o_ref,
              tq=tq, tk=tk, qstart=pl.program_id(1) * tq)


def _mla_attention(qn, qr, kv, wuk, wuv):
    bsz, t, _ = qn.shape
    tq = math.gcd(t, LANES)
    tk = _pick_tile(t, 768)
    kern = functools.partial(_mla_kernel, tq=tq, tk=tk)
    return pl.pallas_call(
        kern,
        grid=(bsz, t // tq),
        in_specs=[pl.BlockSpec((1, tq, qn.shape[2]), lambda b, i: (b, i, 0)),
                  pl.BlockSpec((1, tq, qr.shape[2]), lambda b, i: (b, i, 0)),
                  pl.BlockSpec((1, t, kv.shape[2]), lambda b, i: (b, 0, 0)),
                  pl.BlockSpec(wuk.shape, lambda b, i: (0, 0, 0)),
                  pl.BlockSpec(wuv.shape, lambda b, i: (0, 0, 0))],
        out_specs=pl.BlockSpec((1, tq, HEADS * DK), lambda b, i: (b, i, 0)),
        out_shape=jax.ShapeDtypeStruct((bsz, t, HEADS * DK), F32),
        compiler_params=_params("parallel", "arbitrary"),
        name="mla_attention",
    )(qn, qr, kv, wuk, wuv)


def _mla_paged_kernel(pt_ref, qn_ref, qr_ref, new_ref, wuk_ref, wuv_ref, cache_ref, o_ref, buf, sem,
                      *, layer, n_pages, page, tk):
    slot = _fetch_pages(pt_ref, [(cache_ref, buf, sem, "T", None)], layer=layer, n_pages=n_pages, page=page)
    past = n_pages * page
    _write_tail_t(buf, slot, past, page, new_ref[0])
    _mla_core(qn_ref[0], qr_ref[0], wuk_ref, wuv_ref, lambda sl: buf[slot, :, sl], o_ref,
              tq=qn_ref.shape[1], tk=tk, qstart=past, k_t=True)


def _mla_attention_paged(qn, qr, new_t, wuk, wuv, cache_t, page_table, layer):
    n_pages, page = page_table.shape[1], cache_t.shape[3]
    lp = (n_pages + 1) * page
    kern = functools.partial(_mla_paged_kernel, layer=layer, n_pages=n_pages, page=page, tk=_pick_tile(lp, 768))
    return _paged_call(kern, "mla_attention_paged", page_table, [qn, qr, new_t], [wuk, wuv], [cache_t],
                       _page_buffers_t(lp, cache_t.shape[2]), HEADS * DK)


def _dsa_core(q, qi, wi, kv_get, o_ref, sel_ref, *, tq, tk, lk, qstart, topk, k_t=False):
    n_tiles = (qstart + tq - 1) // tk + 1
    qpos1 = qstart + lax.broadcasted_iota(I32, (tq, 1), 0)
    qpos = _tile_rows(qpos1, HEADS)

    qis = _stack_heads(qi, DK).astype(BF16)
    wi = wi * IDX_HEADS ** -0.5
    tiles = []
    for j in range(lk // tk):
        kidx = kv_get(pl.ds(j * tk, tk), 2 * DK, 3 * DK)
        rel = jnp.maximum(_qk(qis, kidx, k_t) * DK ** -0.5, 0.0)
        sc = wi[:, 0:1] * rel[0:tq]
        for h in range(1, IDX_HEADS):
            sc = sc + wi[:, h:h + 1] * rel[h * tq:(h + 1) * tq]
        tiles.append(sc)
    score = jnp.concatenate(tiles, axis=-1)
    causal = lax.broadcasted_iota(I32, (1, lk), 1) <= qpos1
    sel_ref[...] = _select_topk(score, causal, jnp.full((tq, 1), topk, I32))

    qs = _stack_heads(q, DK).astype(BF16)
    scale = DK ** -0.5

    def body(j, carry):
        sl = pl.ds(pl.multiple_of(j * tk, tk), tk)
        s = _qk(qs, kv_get(sl, 0, DK), k_t) * scale
        kpos = j * tk + lax.broadcasted_iota(I32, (1, tk), 1)
        mask = jnp.where(kpos <= qpos, _tile_rows(sel_ref[:, sl], HEADS), 0.0) > 0.5
        return _softmax_step(s, mask, kv_get(sl, DK, 2 * DK), carry, v_t=k_t)

    o = _softmax_done(lax.fori_loop(0, n_tiles, body, _softmax_init(HEADS * tq, DK)))
    for h in range(HEADS):
        o_ref[0, :, h * DK:(h + 1) * DK] = o[h * tq:(h + 1) * tq]


def _dsa_kernel(q_ref, qi_ref, wi_ref, kv_ref, o_ref, sel_ref, *, tq, tk, q0, topk):
    _dsa_core(q_ref[0], qi_ref[0], wi_ref[0], lambda sl, c0, c1: kv_ref[0, sl, c0:c1], o_ref, sel_ref,
              tq=tq, tk=tk, lk=kv_ref.shape[1], qstart=q0 + pl.program_id(1) * tq, topk=topk)


def _dsa_attention(q, qi, wi, kv, topk):
    bsz, t, _ = q.shape
    tq = math.gcd(t, LANES)
    tk = _pick_tile(t, 512)
    per_seg = tk // tq
    outs = []
    for seg in range(t // tk):
        lk = (seg + 1) * tk
        qmap = lambda b, i, seg=seg: (b, seg * per_seg + i, 0)
        kern = functools.partial(_dsa_kernel, tq=tq, tk=tk, q0=seg * tk, topk=topk)
        outs.append(pl.pallas_call(
            kern,
            grid=(bsz, per_seg),
            in_specs=[pl.BlockSpec((1, tq, HEADS * DK), qmap),
                      pl.BlockSpec((1, tq, HEADS * DK), qmap),
                      pl.BlockSpec((1, tq, IDX_HEADS), qmap),
                      pl.BlockSpec((1, lk, kv.shape[2]), lambda b, i: (b, 0, 0))],
            out_specs=pl.BlockSpec((1, tq, HEADS * DK), lambda b, i: (b, i, 0)),
            out_shape=jax.ShapeDtypeStruct((bsz, tk, HEADS * DK), F32),
            scratch_shapes=[pltpu.VMEM((tq, lk), F32)],
            compiler_params=_params("parallel", "arbitrary"),
            name="dsa_attention",
        )(q, qi, wi, kv))
    return jnp.concatenate(outs, axis=1)


def _dsa_paged_kernel(pt_ref, q_ref, qi_ref, wi_ref, new_ref, cache_ref, o_ref, buf, sem, sel_ref,
                      *, layer, n_pages, page, tk, topk):
    slot = _fetch_pages(pt_ref, [(cache_ref, buf, sem, "T", None)], layer=layer, n_pages=n_pages, page=page)
    past = n_pages * page
    _write_tail_t(buf, slot, past, page, new_ref[0])
    _dsa_core(q_ref[0], qi_ref[0], wi_ref[0], lambda sl, c0, c1: buf[slot, c0:c1, sl], o_ref, sel_ref,
              tq=q_ref.shape[1], tk=tk, lk=past + page, qstart=past, topk=topk, k_t=True)


def _dsa_attention_paged(q, qi, wi, new_t, cache_t, page_table, layer, topk):
    n_pages, page = page_table.shape[1], cache_t.shape[3]
    lp = (n_pages + 1) * page
    kern = functools.partial(_dsa_paged_kernel, layer=layer, n_pages=n_pages, page=page,
                             tk=_pick_tile(lp, 768), topk=topk)
    scratch = _page_buffers_t(lp, cache_t.shape[2]) + [pltpu.VMEM((q.shape[1], lp), F32)]
    return _paged_call(kern, "dsa_attention_paged", page_table, [q, qi, wi, new_t], [], [cache_t],
                       scratch, HEADS * DK)


def _compress_core(chunk_rows, n_ch, w1_ref, pek_ref, pev_ref, wk1_ref, wv1_ref, wk2_ref, wv2_ref, out_ref):
    acc = jnp.zeros((n_ch, 4 * DK), F32)
    for r in range(CMP_STRIDE):
        acc = acc + _dot(chunk_rows(r), w1_ref[r])
    nxt = pltpu.roll(acc, n_ch - 1, 0)
    hk = acc[:, 0:DK] + nxt[:, DK:2 * DK] + jnp.sum(pek_ref[...] * wk1_ref[...], axis=0, keepdims=True)
    hv = acc[:, 2 * DK:3 * DK] + nxt[:, 3 * DK:4 * DK] + jnp.sum(pev_ref[...] * wv1_ref[...], axis=0, keepdims=True)
    out_ref[...] = jnp.zeros_like(out_ref)
    out_ref[0:n_ch, 0:DK] = _dot(hk * jax.nn.sigmoid(hk), wk2_ref[...])
    out_ref[0:n_ch, DK:2 * DK] = _dot(hv * jax.nn.sigmoid(hv), wv2_ref[...])


def _compress_kernel(x_ref, w1_ref, pek_ref, pev_ref, wk1_ref, wv1_ref, wk2_ref, wv2_ref, o_ref, *, n_ch):
    _compress_core(lambda r: x_ref[0, pl.ds(r, n_ch, stride=CMP_STRIDE), :], n_ch,
                   w1_ref, pek_ref, pev_ref, wk1_ref, wv1_ref, wk2_ref, wv2_ref, o_ref.at[0])


def _nsa_compress(kv, cw):
    bsz, lp, _ = kv.shape
    n_ch = lp // CMP_STRIDE
    ncp = _round_up(n_ch, LANES)
    full = lambda a: pl.BlockSpec(a.shape, lambda b: (0,) * a.ndim)
    kern = functools.partial(_compress_kernel, n_ch=n_ch)
    return pl.pallas_call(
        kern,
        grid=(bsz,),
        in_specs=[pl.BlockSpec((1, lp, 2 * DK), lambda b: (b, 0, 0))] + [full(a) for a in cw],
        out_specs=pl.BlockSpec((1, ncp, 2 * DK), lambda b: (b, 0, 0)),
        out_shape=jax.ShapeDtypeStruct((bsz, ncp, 2 * DK), F32),
        compiler_params=_params("parallel"),
        name="nsa_compress",
    )(kv, *cw)


def _nsa_core(q, g, cmp_ref, slc_get, win_get, o_ref, *, tq, tk, tw, qstart, wpos0, n_sel):
    ncp = cmp_ref.shape[0]
    nsp = _round_up(-(-ncp * CMP_STRIDE // SLC_BLOCK) + 1, LANES)
    qpos1 = qstart + lax.broadcasted_iota(I32, (tq, 1), 0)
    qpos = _tile_rows(qpos1, HEADS)
    qs = _stack_heads(q, DK).astype(BF16)
    scale = DK ** -0.5
    rows = HEADS * tq

    s_c = _dot_nt(qs, cmp_ref[:, 0:DK]) * scale
    cmp_end = lax.broadcasted_iota(I32, (1, ncp), 1) * CMP_STRIDE + (CMP_BLOCK - 1)
    m_c = cmp_end <= qpos
    sm = jnp.where(m_c, s_c, NEG)
    e = jnp.where(m_c, jnp.exp(sm - jnp.max(sm, axis=-1, keepdims=True)), 0.0)
    p_c = e / jnp.maximum(jnp.sum(e, axis=-1, keepdims=True), SOFTMAX_FLOOR)
    o_c = _dot(p_c, cmp_ref[:, DK:2 * DK])

    imp = p_c[0:tq]
    for h in range(1, HEADS):
        imp = imp + p_c[h * tq:(h + 1) * tq]
    r_s = SLC_BLOCK // CMP_STRIDE
    r_c = CMP_BLOCK // CMP_STRIDE
    ci = lax.broadcasted_iota(I32, (ncp, nsp), 0)
    bj = lax.broadcasted_iota(I32, (ncp, nsp), 1) * r_s
    overlap = jnp.where(ci >= bj - (r_c - 1), jnp.where(ci <= bj + (r_s - 1), 1.0, 0.0), 0.0)
    imp_s = _dot_split3(imp, overlap.astype(BF16))
    blk = lax.broadcasted_iota(I32, (1, nsp), 1)
    cur = jnp.right_shift(qpos1, int(math.log2(SLC_BLOCK)))
    forced = (blk == 0) | (blk == cur) | (blk == cur - 1)
    cand = (blk <= cur) & jnp.logical_not(forced)
    chosen = _select_topk(imp_s, cand, n_sel - _count(forced))
    sel = jnp.where(forced, 1.0, chosen).astype(BF16)

    def slc_body(j, carry):
        k, v = slc_get(pl.ds(pl.multiple_of(j * tk, tk), tk))
        s = _dot_nt(qs, k) * scale
        kpos = j * tk + lax.broadcasted_iota(I32, (1, tk), 1)
        kblk = jnp.right_shift(j * tk + lax.broadcasted_iota(I32, (nsp, tk), 1), int(math.log2(SLC_BLOCK)))
        expand = jnp.where(lax.broadcasted_iota(I32, (nsp, tk), 0) == kblk, 1.0, 0.0).astype(BF16)
        picked = jnp.dot(sel, expand, preferred_element_type=F32)
        mask = _tile_rows(jnp.where(kpos <= qpos1, picked, 0.0), HEADS) > 0.5
        return _softmax_step(s, mask, v, carry)

    n_tiles = (qstart + tq - 1) // tk + 1
    o_s = _softmax_done(lax.fori_loop(0, n_tiles, slc_body, _softmax_init(rows, DK)))

    def win_body(j, carry):
        k, v = win_get(pl.ds(pl.multiple_of(j * tw, tw), tw))
        s = _dot_nt(qs, k) * scale
        kpos = wpos0 + j * tw + lax.broadcasted_iota(I32, (1, tw), 1)
        mask = (kpos <= qpos) & (kpos >= qpos - WINDOW) & (kpos >= 0)
        return _softmax_step(s, mask, v, carry)

    lo_t = jnp.maximum(qstart - WINDOW - wpos0, 0) // tw
    hi_t = (qstart + tq - 1 - wpos0) // tw + 1
    o_w = _softmax_done(lax.fori_loop(lo_t, hi_t, win_body, _softmax_init(rows, DK)))

    for h in range(HEADS):
        hs = slice(h * tq, (h + 1) * tq)
        o_ref[0, :, h * DK:(h + 1) * DK] = (g[:, 3 * h:3 * h + 1] * o_c[hs] + g[:, 3 * h + 1:3 * h + 2] * o_s[hs]
                                            + g[:, 3 * h + 2:3 * h + 3] * o_w[hs])


def _nsa_kernel(q_ref, g_ref, kv_ref, cmp_ref, win_ref, o_ref, *, tq, tk, tw, n_sel):
    _nsa_core(q_ref[0], g_ref[0], cmp_ref.at[0],
              lambda sl: (kv_ref[0, sl, 2 * DK:3 * DK], kv_ref[0, sl, 3 * DK:4 * DK]),
              lambda sl: (win_ref[0, sl, 0:DK], win_ref[0, sl, DK:2 * DK]),
              o_ref, tq=tq, tk=tk, tw=tw, qstart=pl.program_id(1) * tq, wpos0=0, n_sel=n_sel)


def _nsa_attention(q, g, kv, cmp, win, n_sel):
    bsz, t, _ = q.shape
    tq = math.gcd(t, LANES)
    tk = _pick_tile(t, 768)
    kern = functools.partial(_nsa_kernel, tq=tq, tk=tk, tw=LANES, n_sel=n_sel)
    return pl.pallas_call(
        kern,
        grid=(bsz, t // tq),
        in_specs=[pl.BlockSpec((1, tq, HEADS * DK), lambda b, i: (b, i, 0)),
                  pl.BlockSpec((1, tq, g.shape[2]), lambda b, i: (b, i, 0)),
                  pl.BlockSpec((1, t, kv.shape[2]), lambda b, i: (b, 0, 0)),
                  pl.BlockSpec((1,) + cmp.shape[1:], lambda b, i: (b, 0, 0)),
                  pl.BlockSpec((1,) + win.shape[1:], lambda b, i: (b, 0, 0))],
        out_specs=pl.BlockSpec((1, tq, HEADS * DK), lambda b, i: (b, i, 0)),
        out_shape=jax.ShapeDtypeStruct((bsz, t, HEADS * DK), F32),
        compiler_params=_params("parallel", "arbitrary"),
        name="nsa_attention",
    )(q, g, kv, cmp, win)


def _nsa_paged_kernel(pt_ref, q_ref, g_ref, new_ref, win_ref, w1_ref, pek_ref, pev_ref, wk1_ref, wv1_ref,
                      wk2_ref, wv2_ref, cache_ref, o_ref, cbuf, csem, sbuf, ssem, cmp_ref,
                      *, layer, n_pages, page, tk, wpos0, n_sel):
    half = 2 * DK
    slot = _fetch_pages(pt_ref, [(cache_ref, cbuf, csem, 0, half), (cache_ref, sbuf, ssem, half, half)],
                        layer=layer, n_pages=n_pages, page=page)
    past = n_pages * page
    new = new_ref[0]
    _write_tail(cbuf, slot, past, page, new[:, 0:half])
    _write_tail(sbuf, slot, past, page, new[:, half:2 * half])
    n_ch = (past + page) // CMP_STRIDE
    _compress_core(lambda r: cbuf[slot, pl.ds(r, n_ch, stride=CMP_STRIDE), :], n_ch,
                   w1_ref, pek_ref, pev_ref, wk1_ref, wv1_ref, wk2_ref, wv2_ref, cmp_ref)
    _nsa_core(q_ref[0], g_ref[0], cmp_ref,
              lambda sl: (sbuf[slot, sl, 0:DK], sbuf[slot, sl, DK:2 * DK]),
              lambda sl: (win_ref[0, sl, 0:DK], win_ref[0, sl, DK:2 * DK]),
              o_ref, tq=q_ref.shape[1], tk=tk, tw=LANES, qstart=past, wpos0=wpos0, n_sel=n_sel)


def _nsa_attention_paged(q, g, new, win, cw, cache, page_table, layer, wpos0, n_sel):
    n_pages, page = page_table.shape[1], cache.shape[2]
    lp = (n_pages + 1) * page
    ncp = _round_up(lp // CMP_STRIDE, LANES)
    kern = functools.partial(_nsa_paged_kernel, layer=layer, n_pages=n_pages, page=page,
                             tk=_pick_tile(lp, 768), wpos0=wpos0, n_sel=n_sel)
    scratch = _page_buffers(lp, 2 * DK) + _page_buffers(lp, 2 * DK) + [pltpu.VMEM((ncp, 2 * DK), F32)]
    return _paged_call(kern, "nsa_attention_paged", page_table, [q, g, new, win], list(cw), [cache],
                       scratch, HEADS * DK)


def _rope(x, pos):
    half = x.shape[-1] // 2
    inv = ROPE_THETA ** (-jnp.arange(half, dtype=F32) / half)
    ang = pos.astype(F32)[:, None] * inv[None, :]
    if x.ndim == 4:
        ang = ang[:, None, :]
    cos, sin = jnp.cos(ang), jnp.sin(ang)
    x1, x2 = x[..., :half], x[..., half:]
    return jnp.concatenate([x1 * cos - x2 * sin, x2 * cos + x1 * sin], axis=-1)


def _rms(x, g):
    return x * lax.rsqrt(jnp.mean(x * x, axis=-1, keepdims=True) + EPS) * g


def _in_splits(d_model):
    return (HEADS * DK, DK, DK, DK, DK, DK, DK, 3 * HEADS,
            HEADS * MLA_NOPE, HEADS * MLA_ROPE, MLA_KV_RANK, MLA_ROPE,
            HEADS * DK, FOX_KV_HEADS * DK, FOX_KV_HEADS * DK, HEADS,
            HEADS * DK, DK, DK, IDX_HEADS * DK, DK, IDX_HEADS,
            N_BRANCH * d_model)


def _token_mixers(x2, bsz, t, pos0, past, lw):
    d_model = x2.shape[1]
    splits = _in_splits(d_model)
    proj = _norm_matmul(x2, lw['g_mix'], lw['w_in'])
    cuts = [0]
    for s in splits:
        cuts.append(cuts[-1] + s)
    (nq, nkc, nvc, nks, nvs, nkw, nvw, ng, mqn, mqr, mc, mkr, fq, fk, fv, ff,
     dq, dk_, dv, iq, ik, iw, mg) = [proj[:, a:b].reshape(bsz, t, b - a) for a, b in zip(cuts[:-1], cuts[1:])]
    pos = pos0 + jnp.arange(t, dtype=I32)
    heads4 = lambda a: a.reshape(bsz, t, HEADS, -1)
    rope_heads = lambda a: _rope(heads4(a), pos).reshape(bsz, t, -1)
    rows = {
        'nsa': jnp.concatenate([_rope(nkc, pos), nvc, _rope(nks, pos), nvs], axis=-1),
        'mla': jnp.concatenate([_rms(mc, lw['mla_g_ckv']), _rope(mkr, pos)], axis=-1),
        'fox': jnp.concatenate([fk, fv], axis=-1),
        'fox_logf': jax.nn.log_sigmoid(ff + lw['b_fox_f']),
        'dsa': jnp.concatenate([_rope(dk_, pos), dv, _rope(ik, pos)], axis=-1),
    }
    win_rows = jnp.concatenate([_rope(nkw, pos), nvw], axis=-1)
    win_all = win_rows if past is None else jnp.concatenate([past['win'], win_rows], axis=1)
    seq_len = pos0 + t
    rows['win'] = win_all[:, -min(WINDOW, seq_len):]
    win_pad = _round_up(win_all.shape[1], LANES) - win_all.shape[1]
    win_buf = jnp.pad(win_all, ((0, 0), (0, win_pad), (0, 0)))

    n_sel = min(SLC_TOPN, -(-seq_len // SLC_BLOCK))
    topk = min(DSA_TOPK, seq_len // 4)
    q_nsa, g_nsa = rope_heads(nq), jax.nn.sigmoid(ng)
    q_mla_r = rope_heads(mqr)
    q_dsa, q_idx = rope_heads(dq), rope_heads(iq)
    if past is None:
        cmp = _nsa_compress(rows['nsa'], lw['nsa_cw'])
        o_nsa = _nsa_attention(q_nsa, g_nsa, rows['nsa'], cmp, win_buf, n_sel)
        o_mla = _mla_attention(mqn, q_mla_r, rows['mla'], lw['mla_w_uk'], lw['mla_w_uv'])
        o_fox = _fox_attention(fq, rows['fox_logf'], rows['fox'])
        o_dsa = _dsa_attention(q_dsa, q_idx, iw, rows['dsa'], topk)
    else:
        pt, layer = past['page_table'], past['layer']
        o_nsa = _nsa_attention_paged(q_nsa, g_nsa, rows['nsa'], win_buf, lw['nsa_cw'], past['nsa'], pt, layer,
                                     pos0 - past['win'].shape[1], n_sel)
        new_t = lambda n: jnp.swapaxes(rows[n], 1, 2)
        o_mla = _mla_attention_paged(mqn, q_mla_r, new_t('mla'), lw['mla_w_uk'], lw['mla_w_uv'], past['mla_t'], pt, layer)
        o_fox = _fox_attention_paged(fq, rows['fox'], new_t('fox_logf'), past['fox'], past['fox_logf_t'], pt, layer)
        o_dsa = _dsa_attention_paged(q_dsa, q_idx, iw, new_t('dsa'), past['dsa_t'], pt, layer, topk)
    outs = [o.reshape(bsz * t, HEADS * DK) for o in (o_nsa, o_mla, o_fox, o_dsa)]
    return _merge(x2, outs, mg.reshape(bsz * t, -1), lw['w_branch'], lw['w_out']), rows


def _trunk_layer(x2, bsz, t, pos0, past, lw):
    x2, rows = _token_mixers(x2, bsz, t, pos0, past, lw)
    return _mlp(x2, lw['g_mlp'], lw['w_up'], lw['w_down']), rows


def _compress_weights(wk1, wv1):
    hidden = wk1.shape[1]
    k = wk1.reshape(2, CMP_STRIDE, DK, hidden)
    v = wv1.reshape(2, CMP_STRIDE, DK, hidden)
    z = jnp.zeros_like(k[0])
    top = jnp.concatenate([k[0], k[1], z, z], axis=-1)
    bot = jnp.concatenate([z, z, v[0], v[1]], axis=-1)
    return jnp.concatenate([top, bot], axis=1).astype(BF16)


def kernel(x_prompt, x_sample, cache_nsa, cache_mla, cache_fox, cache_fox_logf, cache_dsa, state_nsa_win, page_table, g_mix, w_in, b_fox_f, nsa_pe_k, nsa_pe_v, nsa_wk1, nsa_wk2, nsa_wv1, nsa_wv2, mla_g_ckv, mla_w_uk, mla_w_uv, w_branch, w_out, g_mlp, w_up, w_down, g_final):
    depth = w_in.shape[0]
    bp, tp, d_model = x_prompt.shape
    bs, ts, _ = x_sample.shape
    past_len = page_table.shape[1] * cache_nsa.shape[2]
    d_in = w_in.shape[2]
    d_in_pad = _round_up(d_in, 512)
    pages_t = lambda cache: jnp.swapaxes(cache, 2, 3)
    names = ('nsa', 'mla', 'fox', 'fox_logf', 'dsa', 'win')
    new_p = {n: [] for n in names}
    new_s = {n: [] for n in names}
    xp = x_prompt.reshape(bp * tp, d_model)
    xs = x_sample.reshape(bs * ts, d_model)
    for l in range(depth):
        lw = {
            'g_mix': g_mix[l],
            'w_in': jnp.pad(w_in[l], ((0, 0), (0, d_in_pad - d_in))).astype(BF16),
            'b_fox_f': b_fox_f[l],
            'nsa_cw': (_compress_weights(nsa_wk1[l], nsa_wv1[l]),
                       nsa_pe_k[l].reshape(-1, 1), nsa_pe_v[l].reshape(-1, 1), nsa_wk1[l], nsa_wv1[l],
                       nsa_wk2[l].astype(BF16), nsa_wv2[l].astype(BF16)),
            'mla_g_ckv': mla_g_ckv[l],
            'mla_w_uk': jnp.transpose(mla_w_uk[l], (1, 2, 0)).astype(BF16),
            'mla_w_uv': jnp.transpose(mla_w_uv[l], (1, 0, 2)).astype(BF16),
            'w_branch': w_branch[l].astype(BF16), 'w_out': w_out[l].astype(BF16),
            'g_mlp': g_mlp[l], 'w_up': w_up[l].astype(BF16), 'w_down': w_down[l].astype(BF16),
        }
        past = {'layer': l, 'page_table': page_table, 'nsa': cache_nsa, 'mla_t': pages_t(cache_mla),
                'fox': cache_fox, 'fox_logf_t': pages_t(cache_fox_logf), 'dsa_t': pages_t(cache_dsa),
                'win': state_nsa_win[l]}
        xp, rows_p = _trunk_layer(xp, bp, tp, 0, None, lw)
        xs, rows_s = _trunk_layer(xs, bs, ts, past_len, past, lw)
        for n in names:
            new_p[n].append(rows_p[n])
            new_s[n].append(rows_s[n])
    y_prompt = _rms_norm_rows(xp, g_final).reshape(bp, tp, d_model)
    y_sample = _rms_norm_rows(xs, g_final).reshape(bs, ts, d_model)
    return (y_prompt, y_sample,
            jnp.stack(new_p['nsa']), jnp.stack(new_p['mla']), jnp.stack(new_p['fox']),
            jnp.stack(new_p['fox_logf']), jnp.stack(new_p['dsa']), jnp.stack(new_p['win']),
            jnp.stack(new_s['nsa']), jnp.stack(new_s['mla']), jnp.stack(new_s['fox']),
            jnp.stack(new_s['fox_logf']), jnp.stack(new_s['dsa']), jnp.stack(new_s['win']))
```

```python
import functools
import math

import jax
import jax.numpy as jnp
from jax import lax
from jax.experimental import pallas as pl
from jax.experimental.pallas import tpu as pltpu

F32 = jnp.float32
BF16 = jnp.bfloat16
I32 = jnp.int32

ROPE_THETA = 10000.0
EPS = 1e-6
N_BRANCH = 4
HEADS = 4
DK = 64
CMP_BLOCK = 32
CMP_STRIDE = 16
SLC_BLOCK = 64
SLC_TOPN = 16
WINDOW = 512
MLA_KV_RANK = 256
MLA_NOPE = 64
MLA_ROPE = 32
FOX_KV_HEADS = 2
DSA_TOPK = 256
IDX_HEADS = 4

LANES = 128
NEG = -1e30
BIG = 3e38
SOFTMAX_FLOOR = 1e-30
VMEM_LIMIT = 56 * 1024 * 1024


def _round_up(x, m):
    return -(-x // m) * m


def _pick_tile(n, cap):
    best = LANES
    for t in range(LANES, cap + 1, LANES):
        if n % t == 0:
            best = t
    return best


def _params(*sem):
    return pltpu.CompilerParams(dimension_semantics=sem, vmem_limit_bytes=VMEM_LIMIT)


def _dot(a, b):
    return jnp.dot(a.astype(BF16), b.astype(BF16), preferred_element_type=F32)


def _dot_nt(a, b):
    return lax.dot_general(a.astype(BF16), b.astype(BF16), (((1,), (1,)), ((), ())),
                           preferred_element_type=F32)


def _split3(a):
    hi = a.astype(BF16)
    r1 = a - hi.astype(F32)
    mid = r1.astype(BF16)
    lo = (r1 - mid.astype(F32)).astype(BF16)
    return hi, mid, lo


def _dot_split3(a, b01):
    hi, mid, lo = _split3(a)
    out = jnp.dot(hi, b01, preferred_element_type=F32)
    out = out + jnp.dot(mid, b01, preferred_element_type=F32)
    return out + jnp.dot(lo, b01, preferred_element_type=F32)


def _stack_heads(x, width):
    return jnp.concatenate([x[:, h * width:(h + 1) * width] for h in range(HEADS)], axis=0)


def _tile_rows(x, n):
    return jnp.concatenate([x] * n, axis=0)


def _softmax_step(s, mask, v, carry, v_t=False):
    m, l, acc = carry
    sm = jnp.where(mask, s, NEG)
    m_new = jnp.maximum(m, jnp.max(sm, axis=-1, keepdims=True))
    p = jnp.where(mask, jnp.exp(sm - m_new), 0.0)
    alpha = jnp.exp(m - m_new)
    l = alpha * l + jnp.sum(p, axis=-1, keepdims=True)
    acc = alpha * acc + (_dot_nt(p, v) if v_t else _dot(p, v))
    return m_new, l, acc


def _qk(q, k, k_t):
    return _dot(q, k) if k_t else _dot_nt(q, k)


def _softmax_init(rows, dv):
    return (jnp.full((rows, 1), NEG, F32), jnp.zeros((rows, 1), F32), jnp.zeros((rows, dv), F32))


def _softmax_done(carry):
    _, l, acc = carry
    return acc / jnp.maximum(l, SOFTMAX_FLOOR)


def _count(pred):
    return jnp.sum(jnp.where(pred, 1, 0).astype(I32), axis=-1, keepdims=True)


def _select_topk(score, cand, k, n_bisect=24):
    rows, n = score.shape
    sc = jnp.where(cand, score, -BIG)
    ncand = _count(cand)
    small = ncand <= k
    lo0 = jnp.min(jnp.where(cand, score, BIG), axis=-1, keepdims=True)
    top = jnp.max(sc, axis=-1, keepdims=True)
    done0 = jnp.where(small, 1.0, jnp.where(_count(sc >= top) >= k, 1.0, 0.0))

    def bisect(_, c):
        lo, hi = c
        mid = 0.5 * (lo + hi)
        ge = _count(sc >= mid) >= k
        return jnp.where(ge, mid, lo), jnp.where(ge, hi, mid)

    _, hi = lax.fori_loop(0, n_bisect, bisect, (lo0, top))

    def pending(c):
        return jnp.min(c[2]) < 0.5

    def step_down(c):
        hi, t, done = c
        nxt = jnp.max(jnp.where(sc < hi, sc, -BIG), axis=-1, keepdims=True)
        fin = _count(sc >= nxt) >= k
        is_done = done > 0.5
        t = jnp.where(is_done, t, nxt)
        hi = jnp.where(is_done, hi, nxt)
        done = jnp.where(fin, 1.0, done)
        return hi, t, done

    _, t, _ = lax.while_loop(pending, step_down, (hi, top, done0))

    above = sc > t
    need = k - _count(above)
    idx = lax.broadcasted_iota(I32, (rows, n), 1)
    tie_idx = jnp.where(sc == t, idx, jnp.int32(1 << 30))
    idx_bits = max(1, (n - 1).bit_length())
    surplus = jnp.where(small, 0, _count(sc == t) - need)

    def idx_search(i, j):
        c = j | jnp.left_shift(jnp.int32(1), idx_bits - 1 - i)
        return jnp.where(_count(tie_idx < c) < need, c, j)

    j = lax.cond(jnp.max(surplus.astype(F32)) > 0.5,
                 lambda: lax.fori_loop(0, idx_bits, idx_search, jnp.zeros((rows, 1), I32)),
                 lambda: jnp.full((rows, 1), n, I32))
    chosen = jnp.where(above, 1.0, jnp.where(tie_idx <= j, 1.0, 0.0))
    return jnp.where(small, jnp.where(cand, 1.0, 0.0), chosen)


def _norm_matmul_kernel(x_ref, g_ref, w_ref, o_ref, h_ref):
    @pl.when(pl.program_id(1) == 0)
    def _():
        x = x_ref[...]
        y = x * lax.rsqrt(jnp.mean(x * x, axis=-1, keepdims=True) + EPS)
        h_ref[...] = (y * g_ref[...]).astype(BF16)

    o_ref[...] = jnp.dot(h_ref[...], w_ref[...], preferred_element_type=F32)


def _norm_matmul(x, g, w):
    m, d = x.shape
    n = w.shape[1]
    tm = math.gcd(m, 1024)
    tn = _pick_tile(n, 512)
    return pl.pallas_call(
        _norm_matmul_kernel,
        grid=(m // tm, n // tn),
        in_specs=[pl.BlockSpec((tm, d), lambda i, j: (i, 0)),
                  pl.BlockSpec((1, d), lambda i, j: (0, 0)),
                  pl.BlockSpec((d, tn), lambda i, j: (0, j))],
        out_specs=pl.BlockSpec((tm, tn), lambda i, j: (i, j)),
        out_shape=jax.ShapeDtypeStruct((m, n), F32),
        scratch_shapes=[pltpu.VMEM((tm, d), BF16)],
        compiler_params=_params("parallel", "arbitrary"),
        name="norm_matmul",
    )(x, g.reshape(1, d), w)


def _merge_kernel(x_ref, o0_ref, o1_ref, o2_ref, o3_ref, mg_ref, wb_ref, wo_ref, y_ref):
    d = x_ref.shape[1]
    merged = None
    for m, o_ref in enumerate((o0_ref, o1_ref, o2_ref, o3_ref)):
        gate = jax.nn.sigmoid(mg_ref[:, m * d:(m + 1) * d])
        term = gate * _dot(o_ref[...], wb_ref[m])
        merged = term if merged is None else merged + term
    y_ref[...] = x_ref[...] + _dot(merged, wo_ref[...])


def _merge(x, outs, mg, wb, wo):
    m, d = x.shape
    bw = outs[0].shape[1]
    tm = math.gcd(m, 256)
    row = lambda i: (i, 0)
    return pl.pallas_call(
        _merge_kernel,
        grid=(m // tm,),
        in_specs=[pl.BlockSpec((tm, d), row)] + [pl.BlockSpec((tm, bw), row)] * N_BRANCH
        + [pl.BlockSpec((tm, N_BRANCH * d), row),
           pl.BlockSpec((N_BRANCH, bw, d), lambda i: (0, 0, 0)),
           pl.BlockSpec((d, d), lambda i: (0, 0))],
        out_specs=pl.BlockSpec((tm, d), row),
        out_shape=jax.ShapeDtypeStruct((m, d), F32),
        compiler_params=_params("parallel"),
        name="merge",
    )(x, *outs, mg, wb, wo)


def _mlp_kernel(x_ref, g_ref, wu_ref, wd_ref, y_ref, h_ref, acc_ref):
    f = pl.program_id(1)

    @pl.when(f == 0)
    def _():
        x = x_ref[...]
        y = x * lax.rsqrt(jnp.mean(x * x, axis=-1, keepdims=True) + EPS)
        h_ref[...] = (y * g_ref[...]).astype(BF16)
        acc_ref[...] = x

    u = jnp.dot(h_ref[...], wu_ref[...], preferred_element_type=F32)
    acc_ref[...] += _dot(jnp.square(jnp.maximum(u, 0.0)), wd_ref[...])

    @pl.when(f == pl.num_programs(1) - 1)
    def _():
        y_ref[...] = acc_ref[...]


def _mlp(x, g, wu, wd):
    m, d = x.shape
    dff = wu.shape[1]
    tm = math.gcd(m, 1024)
    tf = _pick_tile(dff, 512)
    return pl.pallas_call(
        _mlp_kernel,
        grid=(m // tm, dff // tf),
        in_specs=[pl.BlockSpec((tm, d), lambda i, f: (i, 0)),
                  pl.BlockSpec((1, d), lambda i, f: (0, 0)),
                  pl.BlockSpec((d, tf), lambda i, f: (0, f)),
                  pl.BlockSpec((tf, d), lambda i, f: (f, 0))],
        out_specs=pl.BlockSpec((tm, d), lambda i, f: (i, 0)),
        out_shape=jax.ShapeDtypeStruct((m, d), F32),
        scratch_shapes=[pltpu.VMEM((tm, d), BF16), pltpu.VMEM((tm, d), F32)],
        compiler_params=_params("parallel", "arbitrary"),
        name="mlp",
    )(x, g.reshape(1, d), wu, wd)


def _rms_kernel(x_ref, g_ref, y_ref):
    x = x_ref[...]
    y_ref[...] = x * lax.rsqrt(jnp.mean(x * x, axis=-1, keepdims=True) + EPS) * g_ref[...]


def _rms_norm_rows(x, g):
    m, d = x.shape
    tm = math.gcd(m, 1024)
    return pl.pallas_call(
        _rms_kernel,
        grid=(m // tm,),
        in_specs=[pl.BlockSpec((tm, d), lambda i: (i, 0)), pl.BlockSpec((1, d), lambda i: (0, 0))],
        out_specs=pl.BlockSpec((tm, d), lambda i: (i, 0)),
        out_shape=jax.ShapeDtypeStruct((m, d), F32),
        compiler_params=_params("parallel"),
        name="final_norm",
    )(x, g.reshape(1, d))


def _fetch_pages(pt_ref, streams, *, layer, n_pages, page):
    b = pl.program_id(0)
    slot = lax.rem(b, 2)

    def copies(seq, slot):
        out = []
        for cache_ref, buf_ref, sem_ref, col0, width in streams:
            for p in range(n_pages):
                src = cache_ref.at[layer, pt_ref[seq, p]]
                if col0 == "T":
                    dst = buf_ref.at[slot, :, pl.ds(p * page, page)]
                else:
                    dst = buf_ref.at[slot, pl.ds(p * page, page)]
                    if col0 is not None:
                        src = src.at[:, pl.ds(col0, width)]
                out.append(pltpu.make_async_copy(src, dst, sem_ref.at[slot]))
        return out

    @pl.when(b == 0)
    def _():
        for c in copies(0, 0):
            c.start()

    @pl.when(b + 1 < pl.num_programs(0))
    def _():
        for c in copies(b + 1, 1 - slot):
            c.start()

    for c in copies(b, slot):
        c.wait()
    return slot


def _write_tail(buf_ref, slot, start, page, new):
    buf_ref[slot, pl.ds(start, page), :] = jnp.zeros((page, buf_ref.shape[2]), F32)
    buf_ref[slot, pl.ds(start, new.shape[0]), :] = new


def _write_tail_t(buf_ref, slot, start, page, new_t):
    buf_ref[slot, :, pl.ds(start, page)] = jnp.zeros((buf_ref.shape[1], page), F32)
    buf_ref[slot, :, pl.ds(start, new_t.shape[1])] = new_t


def _paged_call(kern, name, page_table, blocked, resident, hbm, scratch, out_width):
    bsz = page_table.shape[0]
    t = blocked[0].shape[1]
    seq = lambda a: pl.BlockSpec((1,) + a.shape[1:], lambda b, pt: (b,) + (0,) * (a.ndim - 1))
    full = lambda a: pl.BlockSpec(a.shape, lambda b, pt: (0,) * a.ndim)
    return pl.pallas_call(
        kern,
        grid_spec=pltpu.PrefetchScalarGridSpec(
            num_scalar_prefetch=1,
            grid=(bsz,),
            in_specs=[seq(a) for a in blocked] + [full(a) for a in resident]
            + [pl.BlockSpec(memory_space=pl.ANY)] * len(hbm),
            out_specs=pl.BlockSpec((1, t, out_width), lambda b, pt: (b, 0, 0)),
            scratch_shapes=scratch),
        out_shape=jax.ShapeDtypeStruct((bsz, t, out_width), F32),
        compiler_params=_params("arbitrary"),
        name=name,
    )(page_table, *blocked, *resident, *hbm)


def _page_buffers(rows, width):
    return [pltpu.VMEM((2, rows, width), F32), pltpu.SemaphoreType.DMA((2,))]


def _page_buffers_t(rows, width):
    return [pltpu.VMEM((2, width, rows), F32), pltpu.SemaphoreType.DMA((2,))]


def _fox_cumsum(logf_get, f_ref):
    r = lax.broadcasted_iota(I32, (LANES, LANES), 0)
    c = lax.broadcasted_iota(I32, (LANES, LANES), 1)
    lower = jnp.where(c <= r, 1.0, 0.0).astype(BF16)

    def body(p, carry):
        sl = pl.ds(pl.multiple_of(p * LANES, LANES), LANES)
        hi, mid, lo = _split3(logf_get(sl))
        w = jnp.dot(lower, hi, preferred_element_type=F32)
        w = w + jnp.dot(lower, mid, preferred_element_type=F32)
        w = w + jnp.dot(lower, lo, preferred_element_type=F32) + carry
        f_ref[sl, :] = w
        return w[LANES - 1:LANES, :]

    lax.fori_loop(0, f_ref.shape[0] // LANES, body, jnp.zeros((1, f_ref.shape[1]), F32))


def _fox_cumsum_lanes(logf_get, f_ref):
    r = lax.broadcasted_iota(I32, (LANES, LANES), 0)
    c = lax.broadcasted_iota(I32, (LANES, LANES), 1)
    upper = jnp.where(r <= c, 1.0, 0.0).astype(BF16)

    def body(p, carry):
        sl = pl.ds(pl.multiple_of(p * LANES, LANES), LANES)
        w = _dot_split3(logf_get(sl), upper) + carry
        f_ref[:, sl] = w
        return w[:, LANES - 1:LANES]

    lax.fori_loop(0, f_ref.shape[1] // LANES, body, jnp.zeros((f_ref.shape[0], 1), F32), unroll=True)


def _fox_core(q, fq, fk_get, kv_get, o_ref, *, tq, tk, qstart):
    n_tiles = (qstart + tq - 1) // tk + 1
    group = HEADS // FOX_KV_HEADS
    kvw = FOX_KV_HEADS * DK
    scale = DK ** -0.5
    qpos = _tile_rows(qstart + lax.broadcasted_iota(I32, (tq, 1), 0), group)
    for hk in range(FOX_KV_HEADS):
        heads = range(hk * group, (hk + 1) * group)
        qs = jnp.concatenate([q[:, h * DK:(h + 1) * DK] for h in heads], axis=0)
        fqs = jnp.concatenate([fq[:, h:h + 1] for h in heads], axis=0)

        def body(j, carry, hk=hk, heads=heads, qs=qs, fqs=fqs):
            sl = pl.ds(pl.multiple_of(j * tk, tk), tk)
            k = kv_get(sl, hk * DK, (hk + 1) * DK)
            v = kv_get(sl, kvw + hk * DK, kvw + (hk + 1) * DK)
            s = _dot_nt(qs, k) * scale + (fqs - fk_get(sl, heads))
            kpos = j * tk + lax.broadcasted_iota(I32, (1, tk), 1)
            return _softmax_step(s, kpos <= qpos, v, carry)

        o = _softmax_done(lax.fori_loop(0, n_tiles, body, _softmax_init(group * tq, DK)))
        for g, h in enumerate(heads):
            o_ref[0, :, h * DK:(h + 1) * DK] = o[g * tq:(g + 1) * tq]


def _fox_kernel(q_ref, lf_ref, kv_ref, o_ref, f_ref, *, tq, tk):
    @pl.when(pl.program_id(1) == 0)
    def _():
        _fox_cumsum(lambda sl: lf_ref[0, sl, :], f_ref)

    qstart = pl.program_id(1) * tq
    lane = lax.broadcasted_iota(I32, (tq, HEADS), 1)

    def fk_get(sl, heads):
        onehot = jnp.concatenate([jnp.where(lane == h, 1.0, 0.0) for h in heads], axis=0).astype(BF16)
        fh, fm, fl = _split3(f_ref[sl, :])
        return _dot_nt(onehot, fh) + _dot_nt(onehot, fm) + _dot_nt(onehot, fl)

    _fox_core(q_ref[0], f_ref[pl.ds(pl.multiple_of(qstart, tq), tq), :], fk_get,
              lambda sl, c0, c1: kv_ref[0, sl, c0:c1], o_ref, tq=tq, tk=tk, qstart=qstart)


def _fox_attention(q, logf, kv):
    bsz, t, _ = q.shape
    tq = math.gcd(t, LANES)
    tk = _pick_tile(t, 768)
    kern = functools.partial(_fox_kernel, tq=tq, tk=tk)
    return pl.pallas_call(
        kern,
        grid=(bsz, t // tq),
        in_specs=[pl.BlockSpec((1, tq, HEADS * DK), lambda b, i: (b, i, 0)),
                  pl.BlockSpec((1, t, HEADS), lambda b, i: (b, 0, 0)),
                  pl.BlockSpec((1, t, kv.shape[2]), lambda b, i: (b, 0, 0))],
        out_specs=pl.BlockSpec((1, tq, HEADS * DK), lambda b, i: (b, i, 0)),
        out_shape=jax.ShapeDtypeStruct((bsz, t, HEADS * DK), F32),
        scratch_shapes=[pltpu.VMEM((t, HEADS), F32)],
        compiler_params=_params("parallel", "arbitrary"),
        name="fox_attention",
    )(q, logf, kv)


def _fox_paged_kernel(pt_ref, q_ref, new_ref, newlf_ref, cache_ref, lcache_ref, o_ref,
                      kvbuf, kvsem, lbuf, lsem, f_ref, *, layer, n_pages, page, tk):
    slot = _fetch_pages(pt_ref, [(cache_ref, kvbuf, kvsem, None, None), (lcache_ref, lbuf, lsem, "T", None)],
                        layer=layer, n_pages=n_pages, page=page)
    past = n_pages * page
    tq = q_ref.shape[1]
    _write_tail(kvbuf, slot, past, page, new_ref[0])
    _write_tail_t(lbuf, slot, past, page, newlf_ref[0])
    _fox_cumsum_lanes(lambda sl: lbuf[slot, :, sl], f_ref)
    pick = jnp.where(lax.broadcasted_iota(I32, (tq, page), 0) == lax.broadcasted_iota(I32, (tq, page), 1),
                     1.0, 0.0).astype(BF16)
    fh, fm, fl = _split3(f_ref[:, pl.ds(past, page)])
    fq = _dot_nt(pick, fh) + _dot_nt(pick, fm) + _dot_nt(pick, fl)

    def fk_get(sl, heads):
        ft = f_ref[:, sl]
        return jnp.concatenate([jnp.broadcast_to(ft[h:h + 1, :], (tq, ft.shape[1])) for h in heads], axis=0)

    _fox_core(q_ref[0], fq, fk_get, lambda sl, c0, c1: kvbuf[slot, sl, c0:c1], o_ref,
              tq=tq, tk=tk, qstart=past)


def _fox_attention_paged(q, new, newlf_t, cache, lcache_t, page_table, layer):
    n_pages, page = page_table.shape[1], cache.shape[2]
    lp = (n_pages + 1) * page
    kern = functools.partial(_fox_paged_kernel, layer=layer, n_pages=n_pages, page=page, tk=_pick_tile(lp, 768))
    scratch = _page_buffers(lp, cache.shape[3]) + _page_buffers_t(lp, HEADS) + [pltpu.VMEM((HEADS, lp), F32)]
    return _paged_call(kern, "fox_attention_paged", page_table, [q, new, newlf_t], [], [cache, lcache_t],
                       scratch, HEADS * DK)


def _mla_core(qn, qr, wuk_ref, wuv_ref, kv_get, o_ref, *, tq, tk, qstart, k_t=False):
    n_tiles = (qstart + tq - 1) // tk + 1
    scale = (MLA_NOPE + MLA_ROPE) ** -0.5
    parts = []
    for h in range(HEADS):
        q_lat = _dot(qn[:, h * MLA_NOPE:(h + 1) * MLA_NOPE], wuk_ref[h])
        parts.append(jnp.concatenate([q_lat, qr[:, h * MLA_ROPE:(h + 1) * MLA_ROPE]], axis=-1))
    qs = jnp.concatenate(parts, axis=0).astype(BF16)
    qpos = _tile_rows(qstart + lax.broadcasted_iota(I32, (tq, 1), 0), HEADS)

    def body(j, carry):
        kt = kv_get(pl.ds(pl.multiple_of(j * tk, tk), tk)).astype(BF16)
        s = _qk(qs, kt, k_t) * scale
        kpos = j * tk + lax.broadcasted_iota(I32, (1, tk), 1)
        v = kt[:MLA_KV_RANK, :] if k_t else kt[:, :MLA_KV_RANK]
        return _softmax_step(s, kpos <= qpos, v, carry, v_t=k_t)

    o_lat = _softmax_done(lax.fori_loop(0, n_tiles, body, _softmax_init(HEADS * tq, MLA_KV_RANK)))
    for h in range(HEADS):
        o_ref[0, :, h * DK:(h + 1) * DK] = _dot(o_lat[h * tq:(h + 1) * tq], wuv_ref[h])


def _mla_kernel(qn_ref, qr_ref, kv_ref, wuk_ref, wuv_ref, o_ref, *, tq, tk):
    _mla_core(qn_ref[0], qr_ref[0], wuk_ref, wuv_ref, lambda sl: kv_ref[0, sl, :], o_ref,
              tq=tq, tk=tk, qstart=pl.program_id(1) * tq)


def _mla_attention(qn, qr, kv, wuk, wuv):
    bsz, t, _ = qn.shape
    tq = math.gcd(t, LANES)
    tk = _pick_tile(t, 768)
    kern = functools.partial(_mla_kernel, tq=tq, tk=tk)
    return pl.pallas_call(
        kern,
        grid=(bsz, t // tq),
        in_specs=[pl.BlockSpec((1, tq, qn.shape[2]), lambda b, i: (b, i, 0)),
                  pl.BlockSpec((1, tq, qr.shape[2]), lambda b, i: (b, i, 0)),
                  pl.BlockSpec((1, t, kv.shape[2]), lambda b, i: (b, 0, 0)),
                  pl.BlockSpec(wuk.shape, lambda b, i: (0, 0, 0)),
                  pl.BlockSpec(wuv.shape, lambda b, i: (0, 0, 0))],
        out_specs=pl.BlockSpec((1, tq, HEADS * DK), lambda b, i: (b, i, 0)),
        out_shape=jax.ShapeDtypeStruct((bsz, t, HEADS * DK), F32),
        compiler_params=_params("parallel", "arbitrary"),
        name="mla_attention",
    )(qn, qr, kv, wuk, wuv)


def _mla_paged_kernel(pt_ref, qn_ref, qr_ref, new_ref, wuk_ref, wuv_ref, cache_ref, o_ref, buf, sem,
                      *, layer, n_pages, page, tk):
    slot = _fetch_pages(pt_ref, [(cache_ref, buf, sem, "T", None)], layer=layer, n_pages=n_pages, page=page)
    past = n_pages * page
    _write_tail_t(buf, slot, past, page, new_ref[0])
    _mla_core(qn_ref[0], qr_ref[0], wuk_ref, wuv_ref, lambda sl: buf[slot, :, sl], o_ref,
              tq=qn_ref.shape[1], tk=tk, qstart=past, k_t=True)


def _mla_attention_paged(qn, qr, new_t, wuk, wuv, cache_t, page_table, layer):
    n_pages, page = page_table.shape[1], cache_t.shape[3]
    lp = (n_pages + 1) * page
    kern = functools.partial(_mla_paged_kernel, layer=layer, n_pages=n_pages, page=page, tk=_pick_tile(lp, 768))
    return _paged_call(kern, "mla_attention_paged", page_table, [qn, qr, new_t], [wuk, wuv], [cache_t],
                       _page_buffers_t(lp, cache_t.shape[2]), HEADS * DK)


def _dsa_core(q, qi, wi, kv_get, o_ref, sel_ref, *, tq, tk, lk, qstart, topk, k_t=False):
    n_tiles = (qstart + tq - 1) // tk + 1
    qpos1 = qstart + lax.broadcasted_iota(I32, (tq, 1), 0)
    qpos = _tile_rows(qpos1, HEADS)

    qis = _stack_heads(qi, DK).astype(BF16)
    wi = wi * IDX_HEADS ** -0.5
    tiles = []
    for j in range(lk // tk):
        kidx = kv_get(pl.ds(j * tk, tk), 2 * DK, 3 * DK)
        rel = jnp.maximum(_qk(qis, kidx, k_t) * DK ** -0.5, 0.0)
        sc = wi[:, 0:1] * rel[0:tq]
        for h in range(1, IDX_HEADS):
            sc = sc + wi[:, h:h + 1] * rel[h * tq:(h + 1) * tq]
        tiles.append(sc)
    score = jnp.concatenate(tiles, axis=-1)
    causal = lax.broadcasted_iota(I32, (1, lk), 1) <= qpos1
    sel_ref[...] = _select_topk(score, causal, jnp.full((tq, 1), topk, I32))

    qs = _stack_heads(q, DK).astype(BF16)
    scale = DK ** -0.5

    def body(j, carry):
        sl = pl.ds(pl.multiple_of(j * tk, tk), tk)
        s = _qk(qs, kv_get(sl, 0, DK), k_t) * scale
        kpos = j * tk + lax.broadcasted_iota(I32, (1, tk), 1)
        mask = jnp.where(kpos <= qpos, _tile_rows(sel_ref[:, sl], HEADS), 0.0) > 0.5
        return _softmax_step(s, mask, kv_get(sl, DK, 2 * DK), carry, v_t=k_t)

    o = _softmax_done(lax.fori_loop(0, n_tiles, body, _softmax_init(HEADS * tq, DK)))
    for h in range(HEADS):
        o_ref[0, :, h * DK:(h + 1) * DK] = o[h * tq:(h + 1) * tq]


def _dsa_kernel(q_ref, qi_ref, wi_ref, kv_ref, o_ref, sel_ref, *, tq, tk, q0, topk):
    _dsa_core(q_ref[0], qi_ref[0], wi_ref[0], lambda sl, c0, c1: kv_ref[0, sl, c0:c1], o_ref, sel_ref,
              tq=tq, tk=tk, lk=kv_ref.shape[1], qstart=q0 + pl.program_id(1) * tq, topk=topk)


def _dsa_attention(q, qi, wi, kv, topk):
    bsz, t, _ = q.shape
    tq = math.gcd(t, LANES)
    tk = _pick_tile(t, 512)
    per_seg = tk // tq
    outs = []
    for seg in range(t // tk):
        lk = (seg + 1) * tk
        qmap = lambda b, i, seg=seg: (b, seg * per_seg + i, 0)
        kern = functools.partial(_dsa_kernel, tq=tq, tk=tk, q0=seg * tk, topk=topk)
        outs.append(pl.pallas_call(
            kern,
            grid=(bsz, per_seg),
            in_specs=[pl.BlockSpec((1, tq, HEADS * DK), qmap),
                      pl.BlockSpec((1, tq, HEADS * DK), qmap),
                      pl.BlockSpec((1, tq, IDX_HEADS), qmap),
                      pl.BlockSpec((1, lk, kv.shape[2]), lambda b, i: (b, 0, 0))],
            out_specs=pl.BlockSpec((1, tq, HEADS * DK), lambda b, i: (b, i, 0)),
            out_shape=jax.ShapeDtypeStruct((bsz, tk, HEADS * DK), F32),
            scratch_shapes=[pltpu.VMEM((tq, lk), F32)],
            compiler_params=_params("parallel", "arbitrary"),
            name="dsa_attention",
        )(q, qi, wi, kv))
    return jnp.concatenate(outs, axis=1)


def _dsa_paged_kernel(pt_ref, q_ref, qi_ref, wi_ref, new_ref, cache_ref, o_ref, buf, sem, sel_ref,
                      *, layer, n_pages, page, tk, topk):
    slot = _fetch_pages(pt_ref, [(cache_ref, buf, sem, "T", None)], layer=layer, n_pages=n_pages, page=page)
    past = n_pages * page
    _write_tail_t(buf, slot, past, page, new_ref[0])
    _dsa_core(q_ref[0], qi_ref[0], wi_ref[0], lambda sl, c0, c1: buf[slot, c0:c1, sl], o_ref, sel_ref,
              tq=q_ref.shape[1], tk=tk, lk=past + page, qstart=past, topk=topk, k_t=True)


def _dsa_attention_paged(q, qi, wi, new_t, cache_t, page_table, layer, topk):
    n_pages, page = page_table.shape[1], cache_t.shape[3]
    lp = (n_pages + 1) * page
    kern = functools.partial(_dsa_paged_kernel, layer=layer, n_pages=n_pages, page=page,
                             tk=_pick_tile(lp, 768), topk=topk)
    scratch = _page_buffers_t(lp, cache_t.shape[2]) + [pltpu.VMEM((q.shape[1], lp), F32)]
    return _paged_call(kern, "dsa_attention_paged", page_table, [q, qi, wi, new_t], [], [cache_t],
                       scratch, HEADS * DK)


def _compress_core(chunk_rows, n_ch, w1_ref, pek_ref, pev_ref, wk1_ref, wv1_ref, wk2_ref, wv2_ref, out_ref):
    acc = jnp.zeros((n_ch, 4 * DK), F32)
    for r in range(CMP_STRIDE):
        acc = acc + _dot(chunk_rows(r), w1_ref[r])
    nxt = pltpu.roll(acc, n_ch - 1, 0)
    hk = acc[:, 0:DK] + nxt[:, DK:2 * DK] + jnp.sum(pek_ref[...] * wk1_ref[...], axis=0, keepdims=True)
    hv = acc[:, 2 * DK:3 * DK] + nxt[:, 3 * DK:4 * DK] + jnp.sum(pev_ref[...] * wv1_ref[...], axis=0, keepdims=True)
    out_ref[...] = jnp.zeros_like(out_ref)
    out_ref[0:n_ch, 0:DK] = _dot(hk * jax.nn.sigmoid(hk), wk2_ref[...])
    out_ref[0:n_ch, DK:2 * DK] = _dot(hv * jax.nn.sigmoid(hv), wv2_ref[...])


def _compress_kernel(x_ref, w1_ref, pek_ref, pev_ref, wk1_ref, wv1_ref, wk2_ref, wv2_ref, o_ref, *, n_ch):
    _compress_core(lambda r: x_ref[0, pl.ds(r, n_ch, stride=CMP_STRIDE), :], n_ch,
                   w1_ref, pek_ref, pev_ref, wk1_ref, wv1_ref, wk2_ref, wv2_ref, o_ref.at[0])


def _nsa_compress(kv, cw):
    bsz, lp, _ = kv.shape
    n_ch = lp // CMP_STRIDE
    ncp = _round_up(n_ch, LANES)
    full = lambda a: pl.BlockSpec(a.shape, lambda b: (0,) * a.ndim)
    kern = functools.partial(_compress_kernel, n_ch=n_ch)
    return pl.pallas_call(
        kern,
        grid=(bsz,),
        in_specs=[pl.BlockSpec((1, lp, 2 * DK), lambda b: (b, 0, 0))] + [full(a) for a in cw],
        out_specs=pl.BlockSpec((1, ncp, 2 * DK), lambda b: (b, 0, 0)),
        out_shape=jax.ShapeDtypeStruct((bsz, ncp, 2 * DK), F32),
        compiler_params=_params("parallel"),
        name="nsa_compress",
    )(kv, *cw)


def _nsa_core(q, g, cmp_ref, slc_get, win_get, o_ref, *, tq, tk, tw, qstart, wpos0, n_sel):
    ncp = cmp_ref.shape[0]
    nsp = _round_up(-(-ncp * CMP_STRIDE // SLC_BLOCK) + 1, LANES)
    qpos1 = qstart + lax.broadcasted_iota(I32, (tq, 1), 0)
    qpos = _tile_rows(qpos1, HEADS)
    qs = _stack_heads(q, DK).astype(BF16)
    scale = DK ** -0.5
    rows = HEADS * tq

    s_c = _dot_nt(qs, cmp_ref[:, 0:DK]) * scale
    cmp_end = lax.broadcasted_iota(I32, (1, ncp), 1) * CMP_STRIDE + (CMP_BLOCK - 1)
    m_c = cmp_end <= qpos
    sm = jnp.where(m_c, s_c, NEG)
    e = jnp.where(m_c, jnp.exp(sm - jnp.max(sm, axis=-1, keepdims=True)), 0.0)
    p_c = e / jnp.maximum(jnp.sum(e, axis=-1, keepdims=True), SOFTMAX_FLOOR)
    o_c = _dot(p_c, cmp_ref[:, DK:2 * DK])

    imp = p_c[0:tq]
    for h in range(1, HEADS):
        imp = imp + p_c[h * tq:(h + 1) * tq]
    r_s = SLC_BLOCK // CMP_STRIDE
    r_c = CMP_BLOCK // CMP_STRIDE
    ci = lax.broadcasted_iota(I32, (ncp, nsp), 0)
    bj = lax.broadcasted_iota(I32, (ncp, nsp), 1) * r_s
    overlap = jnp.where(ci >= bj - (r_c - 1), jnp.where(ci <= bj + (r_s - 1), 1.0, 0.0), 0.0)
    imp_s = _dot_split3(imp, overlap.astype(BF16))
    blk = lax.broadcasted_iota(I32, (1, nsp), 1)
    cur = jnp.right_shift(qpos1, int(math.log2(SLC_BLOCK)))
    forced = (blk == 0) | (blk == cur) | (blk == cur - 1)
    cand = (blk <= cur) & jnp.logical_not(forced)
    chosen = _select_topk(imp_s, cand, n_sel - _count(forced))
    sel = jnp.where(forced, 1.0, chosen).astype(BF16)

    def slc_body(j, carry):
        k, v = slc_get(pl.ds(pl.multiple_of(j * tk, tk), tk))
        s = _dot_nt(qs, k) * scale
        kpos = j * tk + lax.broadcasted_iota(I32, (1, tk), 1)
        kblk = jnp.right_shift(j * tk + lax.broadcasted_iota(I32, (nsp, tk), 1), int(math.log2(SLC_BLOCK)))
        expand = jnp.where(lax.broadcasted_iota(I32, (nsp, tk), 0) == kblk, 1.0, 0.0).astype(BF16)
        picked = jnp.dot(sel, expand, preferred_element_type=F32)
        mask = _tile_rows(jnp.where(kpos <= qpos1, picked, 0.0), HEADS) > 0.5
        return _softmax_step(s, mask, v, carry)

    n_tiles = (qstart + tq - 1) // tk + 1
    o_s = _softmax_done(lax.fori_loop(0, n_tiles, slc_body, _softmax_init(rows, DK)))

    def win_body(j, carry):
        k, v = win_get(pl.ds(pl.multiple_of(j * tw, tw), tw))
        s = _dot_nt(qs, k) * scale
        kpos = wpos0 + j * tw + lax.broadcasted_iota(I32, (1, tw), 1)
        mask = (kpos <= qpos) & (kpos >= qpos - WINDOW) & (kpos >= 0)
        return _softmax_step(s, mask, v, carry)

    lo_t = jnp.maximum(qstart - WINDOW - wpos0, 0) // tw
    hi_t = (qstart + tq - 1 - wpos0) // tw + 1
    o_w = _softmax_done(lax.fori_loop(lo_t, hi_t, win_body, _softmax_init(rows, DK)))

    for h in range(HEADS):
        hs = slice(h * tq, (h + 1) * tq)
        o_ref[0, :, h * DK:(h + 1) * DK] = (g[:, 3 * h:3 * h + 1] * o_c[hs] + g[:, 3 * h + 1:3 * h + 2] * o_s[hs]
                                            + g[:, 3 * h + 2:3 * h + 3] * o_w[hs])


def _nsa_kernel(q_ref, g_ref, kv_ref, cmp_ref, win_ref, o_ref, *, tq, tk, tw, n_sel):
    _nsa_core(q_ref[0], g_ref[0], cmp_ref.at[0],
              lambda sl: (kv_ref[0, sl, 2 * DK:3 * DK], kv_ref[0, sl, 3 * DK:4 * DK]),
              lambda sl: (win_ref[0, sl, 0:DK], win_ref[0, sl, DK:2 * DK]),
              o_ref, tq=tq, tk=tk, tw=tw, qstart=pl.program_id(1) * tq, wpos0=0, n_sel=n_sel)


def _nsa_attention(q, g, kv, cmp, win, n_sel):
    bsz, t, _ = q.shape
    tq = math.gcd(t, LANES)
    tk = _pick_tile(t, 768)
    kern = functools.partial(_nsa_kernel, tq=tq, tk=tk, tw=LANES, n_sel=n_sel)
    return pl.pallas_call(
        kern,
        grid=(bsz, t // tq),
        in_specs=[pl.BlockSpec((1, tq, HEADS * DK), lambda b, i: (b, i, 0)),
                  pl.BlockSpec((1, tq, g.shape[2]), lambda b, i: (b, i, 0)),
                  pl.BlockSpec((1, t, kv.shape[2]), lambda b, i: (b, 0, 0)),
                  pl.BlockSpec((1,) + cmp.shape[1:], lambda b, i: (b, 0, 0)),
                  pl.BlockSpec((1,) + win.shape[1:], lambda b, i: (b, 0, 0))],
        out_specs=pl.BlockSpec((1, tq, HEADS * DK), lambda b, i: (b, i, 0)),
        out_shape=jax.ShapeDtypeStruct((bsz, t, HEADS * DK), F32),
        compiler_params=_params("parallel", "arbitrary"),
        name="nsa_attention",
    )(q, g, kv, cmp, win)


def _nsa_paged_kernel(pt_ref, q_ref, g_ref, new_ref, win_ref, w1_ref, pek_ref, pev_ref, wk1_ref, wv1_ref,
                      wk2_ref, wv2_ref, cache_ref, o_ref, cbuf, csem, sbuf, ssem, cmp_ref,
                      *, layer, n_pages, page, tk, wpos0, n_sel):
    half = 2 * DK
    slot = _fetch_pages(pt_ref, [(cache_ref, cbuf, csem, 0, half), (cache_ref, sbuf, ssem, half, half)],
                        layer=layer, n_pages=n_pages, page=page)
    past = n_pages * page
    new = new_ref[0]
    _write_tail(cbuf, slot, past, page, new[:, 0:half])
    _write_tail(sbuf, slot, past, page, new[:, half:2 * half])
    n_ch = (past + page) // CMP_STRIDE
    _compress_core(lambda r: cbuf[slot, pl.ds(r, n_ch, stride=CMP_STRIDE), :], n_ch,
                   w1_ref, pek_ref, pev_ref, wk1_ref, wv1_ref, wk2_ref, wv2_ref, cmp_ref)
    _nsa_core(q_ref[0], g_ref[0], cmp_ref,
              lambda sl: (sbuf[slot, sl, 0:DK], sbuf[slot, sl, DK:2 * DK]),
              lambda sl: (win_ref[0, sl, 0:DK], win_ref[0, sl, DK:2 * DK]),
              o_ref, tq=q_ref.shape[1], tk=tk, tw=LANES, qstart=past, wpos0=wpos0, n_sel=n_sel)


def _nsa_attention_paged(q, g, new, win, cw, cache, page_table, layer, wpos0, n_sel):
    n_pages, page = page_table.shape[1], cache.shape[2]
    lp = (n_pages + 1) * page
    ncp = _round_up(lp // CMP_STRIDE, LANES)
    kern = functools.partial(_nsa_paged_kernel, layer=layer, n_pages=n_pages, page=page,
                             tk=_pick_tile(lp, 768), wpos0=wpos0, n_sel=n_sel)
    scratch = _page_buffers(lp, 2 * DK) + _page_buffers(lp, 2 * DK) + [pltpu.VMEM((ncp, 2 * DK), F32)]
    return _paged_call(kern, "nsa_attention_paged", page_table, [q, g, new, win], list(cw), [cache],
                       scratch, HEADS * DK)


def _rope(x, pos):
    half = x.shape[-1] // 2
    inv = ROPE_THETA ** (-jnp.arange(half, dtype=F32) / half)
    ang = pos.astype(F32)[:, None] * inv[None, :]
    if x.ndim == 4:
        ang = ang[:, None, :]
    cos, sin = jnp.cos(ang), jnp.sin(ang)
    x1, x2 = x[..., :half], x[..., half:]
    return jnp.concatenate([x1 * cos - x2 * sin, x2 * cos + x1 * sin], axis=-1)


def _rms(x, g):
    return x * lax.rsqrt(jnp.mean(x * x, axis=-1, keepdims=True) + EPS) * g


def _in_splits(d_model):
    return (HEADS * DK, DK, DK, DK, DK, DK, DK, 3 * HEADS,
            HEADS * MLA_NOPE, HEADS * MLA_ROPE, MLA_KV_RANK, MLA_ROPE,
            HEADS * DK, FOX_KV_HEADS * DK, FOX_KV_HEADS * DK, HEADS,
            HEADS * DK, DK, DK, IDX_HEADS * DK, DK, IDX_HEADS,
            N_BRANCH * d_model)


def _token_mixers(x2, bsz, t, pos0, past, lw):
    d_model = x2.shape[1]
    splits = _in_splits(d_model)
    proj = _norm_matmul(x2, lw['g_mix'], lw['w_in'])
    cuts = [0]
    for s in splits:
        cuts.append(cuts[-1] + s)
    (nq, nkc, nvc, nks, nvs, nkw, nvw, ng, mqn, mqr, mc, mkr, fq, fk, fv, ff,
     dq, dk_, dv, iq, ik, iw, mg) = [proj[:, a:b].reshape(bsz, t, b - a) for a, b in zip(cuts[:-1], cuts[1:])]
    pos = pos0 + jnp.arange(t, dtype=I32)
    heads4 = lambda a: a.reshape(bsz, t, HEADS, -1)
    rope_heads = lambda a: _rope(heads4(a), pos).reshape(bsz, t, -1)
    rows = {
        'nsa': jnp.concatenate([_rope(nkc, pos), nvc, _rope(nks, pos), nvs], axis=-1),
        'mla': jnp.concatenate([_rms(mc, lw['mla_g_ckv']), _rope(mkr, pos)], axis=-1),
        'fox': jnp.concatenate([fk, fv], axis=-1),
        'fox_logf': jax.nn.log_sigmoid(ff + lw['b_fox_f']),
        'dsa': jnp.concatenate([_rope(dk_, pos), dv, _rope(ik, pos)], axis=-1),
    }
    win_rows = jnp.concatenate([_rope(nkw, pos), nvw], axis=-1)
    win_all = win_rows if past is None else jnp.concatenate([past['win'], win_rows], axis=1)
    seq_len = pos0 + t
    rows['win'] = win_all[:, -min(WINDOW, seq_len):]
    win_pad = _round_up(win_all.shape[1], LANES) - win_all.shape[1]
    win_buf = jnp.pad(win_all, ((0, 0), (0, win_pad), (0, 0)))

    n_sel = min(SLC_TOPN, -(-seq_len // SLC_BLOCK))
    topk = min(DSA_TOPK, seq_len // 4)
    q_nsa, g_nsa = rope_heads(nq), jax.nn.sigmoid(ng)
    q_mla_r = rope_heads(mqr)
    q_dsa, q_idx = rope_heads(dq), rope_heads(iq)
    if past is None:
        cmp = _nsa_compress(rows['nsa'], lw['nsa_cw'])
        o_nsa = _nsa_attention(q_nsa, g_nsa, rows['nsa'], cmp, win_buf, n_sel)
        o_mla = _mla_attention(mqn, q_mla_r, rows['mla'], lw['mla_w_uk'], lw['mla_w_uv'])
        o_fox = _fox_attention(fq, rows['fox_logf'], rows['fox'])
        o_dsa = _dsa_attention(q_dsa, q_idx, iw, rows['dsa'], topk)
    else:
        pt, layer = past['page_table'], past['layer']
        o_nsa = _nsa_attention_paged(q_nsa, g_nsa, rows['nsa'], win_buf, lw['nsa_cw'], past['nsa'], pt, layer,
                                     pos0 - past['win'].shape[1], n_sel)
        new_t = lambda n: jnp.swapaxes(rows[n], 1, 2)
        o_mla = _mla_attention_paged(mqn, q_mla_r, new_t('mla'), lw['mla_w_uk'], lw['mla_w_uv'], past['mla_t'], pt, layer)
        o_fox = _fox_attention_paged(fq, rows['fox'], new_t('fox_logf'), past['fox'], past['fox_logf_t'], pt, layer)
        o_dsa = _dsa_attention_paged(q_dsa, q_idx, iw, new_t('dsa'), past['dsa_t'], pt, layer, topk)
    outs = [o.reshape(bsz * t, HEADS * DK) for o in (o_nsa, o_mla, o_fox, o_dsa)]
    return _merge(x2, outs, mg.reshape(bsz * t, -1), lw['w_branch'], lw['w_out']), rows


def _trunk_layer(x2, bsz, t, pos0, past, lw):
    x2, rows = _token_mixers(x2, bsz, t, pos0, past, lw)
    return _mlp(x2, lw['g_mlp'], lw['w_up'], lw['w_down']), rows


def _compress_weights(wk1, wv1):
    hidden = wk1.shape[1]
    k = wk1.reshape(2, CMP_STRIDE, DK, hidden)
    v = wv1.reshape(2, CMP_STRIDE, DK, hidden)
    z = jnp.zeros_like(k[0])
    top = jnp.concatenate([k[0], k[1], z, z], axis=-1)
    bot = jnp.concatenate([z, z, v[0], v[1]], axis=-1)
    return jnp.concatenate([top, bot], axis=1).astype(BF16)


def kernel(x_prompt, x_sample, cache_nsa, cache_mla, cache_fox, cache_fox_logf, cache_dsa, state_nsa_win, page_table, g_mix, w_in, b_fox_f, nsa_pe_k, nsa_pe_v, nsa_wk1, nsa_wk2, nsa_wv1, nsa_wv2, mla_g_ckv, mla_w_uk, mla_w_uv, w_branch, w_out, g_mlp, w_up, w_down, g_final):
    depth = w_in.shape[0]
    bp, tp, d_model = x_prompt.shape
    bs, ts, _ = x_sample.shape
    past_len = page_table.shape[1] * cache_nsa.shape[2]
    d_in = w_in.shape[2]
    d_in_pad = _round_up(d_in, 512)
    pages_t = lambda cache: jnp.swapaxes(cache, 2, 3)
    names = ('nsa', 'mla', 'fox', 'fox_logf', 'dsa', 'win')
    new_p = {n: [] for n in names}
    new_s = {n: [] for n in names}
    xp = x_prompt.reshape(bp * tp, d_model)
    xs = x_sample.reshape(bs * ts, d_model)
    for l in range(depth):
        lw = {
            'g_mix': g_mix[l],
            'w_in': jnp.pad(w_in[l], ((0, 0), (0, d_in_pad - d_in))).astype(BF16),
            'b_fox_f': b_fox_f[l],
            'nsa_cw': (_compress_weights(nsa_wk1[l], nsa_wv1[l]),
                       nsa_pe_k[l].reshape(-1, 1), nsa_pe_v[l].reshape(-1, 1), nsa_wk1[l], nsa_wv1[l],
                       nsa_wk2[l].astype(BF16), nsa_wv2[l].astype(BF16)),
            'mla_g_ckv': mla_g_ckv[l],
            'mla_w_uk': jnp.transpose(mla_w_uk[l], (1, 2, 0)).astype(BF16),
            'mla_w_uv': jnp.transpose(mla_w_uv[l], (1, 0, 2)).astype(BF16),
            'w_branch': w_branch[l].astype(BF16), 'w_out': w_out[l].astype(BF16),
            'g_mlp': g_mlp[l], 'w_up': w_up[l].astype(BF16), 'w_down': w_down[l].astype(BF16),
        }
        past = {'layer': l, 'page_table': page_table, 'nsa': cache_nsa, 'mla_t': pages_t(cache_mla),
                'fox': cache_fox, 'fox_logf_t': pages_t(cache_fox_logf), 'dsa_t': pages_t(cache_dsa),
                'win': state_nsa_win[l]}
        xp, rows_p = _trunk_layer(xp, bp, tp, 0, None, lw)
        xs, rows_s = _trunk_layer(xs, bs, ts, past_len, past, lw)
        for n in names:
            new_p[n].append(rows_p[n])
            new_s[n].append(rows_s[n])
    y_prompt = _rms_norm_rows(xp, g_final).reshape(bp, tp, d_model)
    y_sample = _rms_norm_rows(xs, g_final).reshape(bs, ts, d_model)
    return (y_prompt, y_sample,
            jnp.stack(new_p['nsa']), jnp.stack(new_p['mla']), jnp.stack(new_p['fox']),
            jnp.stack(new_p['fox_logf']), jnp.stack(new_p['dsa']), jnp.stack(new_p['win']),
            jnp.stack(new_s['nsa']), jnp.stack(new_s['mla']), jnp.stack(new_s['fox']),
            jnp.stack(new_s['fox_logf']), jnp.stack(new_s['dsa']), jnp.stack(new_s['win']))
```
